```python
import math
import jax
import jax.numpy as jnp
from jax import lax
import numpy as np

D_MODEL = 1024
BATCH = 32
SEQ = 256
DEPTH = 4
DEC_BATCH = 4
DEC_SEQ = 4096
PAST_LEN = 256

GRID_W = 64
A_WIDTH = D_MODEL // 2
HEAD_DIM = 64
H_ATT = D_MODEL // 128
KV_HEADS = 2
Q_PER_KV = H_ATT // KV_HEADS
WINDOW = 128
BLOCK = 128
ROPE_BASE = 10000.0
DN_HEADS = 4
DN_DK = 128
DN_DV = 128
DN_CHUNK = 64
DN_CONV_WIDTH = DN_HEADS * (2 * DN_DK + DN_DV)
FF_DENSE = 2816
N_EXPERTS = 8
TOP_K = 2
FF_EXPERT = D_MODEL
N_MOD = 6
N_DENSE = (DEPTH + 1) // 2
N_MOE = DEPTH // 2
IN_WIDTH = 3 * A_WIDTH + (H_ATT + 2 * KV_HEADS) * HEAD_DIM + DN_HEADS * (2 * DN_DK + 2 * DN_DV) + 4 * DN_HEADS + 3 * D_MODEL
EPS = 1e-6
NEG = -1e30

kernel_name = "hybrid_dit_conv_swa_gdn_moe_step"


def proj_split_points():
    sizes = (A_WIDTH, A_WIDTH, A_WIDTH,
             H_ATT * HEAD_DIM, KV_HEADS * HEAD_DIM, KV_HEADS * HEAD_DIM,
             DN_HEADS * DN_DK, DN_HEADS * DN_DK, DN_HEADS * DN_DV, DN_HEADS * DN_DV,
             2 * DN_HEADS, 2 * DN_HEADS)
    points, acc = [], 0
    for s in sizes:
        acc += s
        points.append(acc)
    return points


def rms_norm(x, g):
    xf = x.astype(jnp.float32)
    y = xf * lax.rsqrt(jnp.mean(xf * xf, axis=-1, keepdims=True) + EPS)
    return (y * g.astype(jnp.float32)).astype(x.dtype)


def l2_normalize(x):
    xf = x.astype(jnp.float32)
    return xf * lax.rsqrt(jnp.sum(xf * xf, axis=-1, keepdims=True) + EPS)


def dwconv3(x, w):
    return lax.conv_general_dilated(
        x, w[:, None, :].astype(x.dtype), window_strides=(1,), padding=((1, 1),),
        dimension_numbers=("NWC", "WIO", "NWC"), feature_group_count=x.shape[-1])


def ada_modulation(cvec, w, b):
    m = jax.nn.silu(cvec) @ w + b
    return jnp.split(m[..., None, :], N_MOD, axis=-1)


def modulate(x, g, shift, scale):
    return rms_norm(x, g) * (1 + scale) + shift


def axial_rope_tables(n_tokens):
    n_rows = n_tokens // GRID_W
    row = jnp.repeat(jnp.arange(n_rows, dtype=jnp.float32), GRID_W)
    col = jnp.tile(jnp.arange(GRID_W, dtype=jnp.float32), n_rows)
    n_freq = HEAD_DIM // 4
    inv_freq = ROPE_BASE ** (-jnp.arange(n_freq, dtype=jnp.float32) / n_freq)
    ang = jnp.concatenate([row[:, None] * inv_freq, col[:, None] * inv_freq], axis=-1)
    return jnp.cos(ang), jnp.sin(ang)


def apply_axial_rope(x, cos, sin):
    xf = x.astype(jnp.float32).reshape(x.shape[:-1] + (HEAD_DIM // 2, 2))
    x0, x1 = xf[..., 0], xf[..., 1]
    c, s = cos[:, None, :], sin[:, None, :]
    out = jnp.stack([x0 * c - x1 * s, x0 * s + x1 * c], axis=-1)
    return out.reshape(x.shape).astype(x.dtype)


def attend_with_sink(q, k, v, mask, sink):
    s = jnp.einsum("bqkgd,bskd->bkgqs", q, k, preferred_element_type=jnp.float32) * (HEAD_DIM ** -0.5)
    if mask is not None:
        s = jnp.where(mask, s, NEG)
    s_sink = jnp.broadcast_to(sink.astype(jnp.float32).reshape(KV_HEADS, Q_PER_KV, 1, 1), s.shape[:-1] + (1,))
    p = jax.nn.softmax(jnp.concatenate([s, s_sink], axis=-1), axis=-1)[..., :-1]
    return jnp.einsum("bkgqs,bskd->bqkgd", p.astype(v.dtype), v)


def context_attention(q, k, v, sink):
    B, L = q.shape[:2]
    qb = q.reshape(B, L // BLOCK, BLOCK, KV_HEADS, Q_PER_KV, HEAD_DIM).swapaxes(0, 1)
    o = lax.map(lambda q_n: attend_with_sink(q_n, k, v, None, sink), qb)
    return o.swapaxes(0, 1).reshape(B, L, H_ATT * HEAD_DIM)


def latent_window_attention(q, k, v, ctx_k, ctx_v, sink):
    B, T = q.shape[:2]
    nb = T // BLOCK
    n_ctx = ctx_k.shape[1]
    qb = q.reshape(B, nb, BLOCK, KV_HEADS, Q_PER_KV, HEAD_DIM).swapaxes(0, 1)

    def band(x):
        xp = jnp.pad(x, ((0, 0), (BLOCK, BLOCK), (0, 0), (0, 0))).reshape(B, nb + 2, BLOCK, KV_HEADS, HEAD_DIM)
        return jnp.concatenate([xp[:, :-2], xp[:, 1:-1], xp[:, 2:]], axis=2).swapaxes(0, 1)

    kb, vb = band(k), band(v)
    q_idx = jnp.arange(BLOCK)[:, None]
    s_idx = jnp.arange(3 * BLOCK)[None, :]
    key_pos = jnp.arange(nb)[:, None, None] * BLOCK - BLOCK + s_idx[None]
    in_window = jnp.abs(s_idx - BLOCK - q_idx) <= WINDOW
    local_mask = in_window[None] & (key_pos >= 0) & (key_pos < T)
    ctx_mask = jnp.ones((BLOCK, n_ctx), dtype=bool)

    def one_block(args):
        q_n, k_n, v_n, m_n = args
        keys = jnp.concatenate([k_n, ctx_k], axis=1)
        vals = jnp.concatenate([v_n, ctx_v], axis=1)
        mask = jnp.concatenate([m_n, ctx_mask], axis=1)
        return attend_with_sink(q_n, keys, vals, mask, sink)

    o = lax.map(one_block, (qb, kb, vb, local_mask))
    return o.swapaxes(0, 1).reshape(B, T, H_ATT * HEAD_DIM)


def gated_delta_chunked(q, k, v, g, beta, s0):
    B, T, H, _ = q.shape
    n = T // DN_CHUNK

    def chunks(x):
        x = x.reshape((B, n, DN_CHUNK, H) + x.shape[3:])
        return jnp.moveaxis(x, (1, 3), (0, 2))

    qc, kc, vc, gc, bc = chunks(q), chunks(k), chunks(v), chunks(g), chunks(beta)
    gcum = jnp.cumsum(gc, axis=-1)
    idx = jnp.arange(DN_CHUNK)
    tril = idx[:, None] >= idx[None, :]
    strict = idx[:, None] > idx[None, :]
    decay = jnp.exp(jnp.where(tril, gcum[..., :, None] - gcum[..., None, :], NEG))
    kb = kc * bc[..., None]
    lower = jnp.where(strict, jnp.einsum("nbhid,nbhjd->nbhij", kb, kc) * decay, 0.0)
    a_mat = lower + jnp.eye(DN_CHUNK, dtype=jnp.float32)

    def solve(rhs):
        return lax.linalg.triangular_solve(a_mat, rhs, left_side=True, lower=True, unit_diagonal=True)

    u = solve(vc * bc[..., None])
    w = solve(kb * jnp.exp(gcum)[..., None])
    qk = jnp.where(tril, jnp.einsum("nbhid,nbhjd->nbhij", qc, kc) * decay, 0.0)
    q_dec = qc * jnp.exp(gcum)[..., None]
    g_last = gcum[..., -1]
    k_dec = kc * jnp.exp(g_last[..., None] - gcum)[..., None]

    def step(s, inp):
        q_d, qk_n, u_n, w_n, k_d, gl = inp
        v_new = u_n - jnp.einsum("bhcd,bhde->bhce", w_n, s)
        o = jnp.einsum("bhcd,bhde->bhce", q_d, s) + jnp.einsum("bhij,bhje->bhie", qk_n, v_new)
        s = s * jnp.exp(gl)[..., None, None] + jnp.einsum("bhcd,bhce->bhde", k_d, v_new)
        return s, o

    s_fin, o = lax.scan(step, s0.astype(jnp.float32), (q_dec, qk, u, w, k_dec, g_last))
    o = jnp.moveaxis(o, (0, 2), (1, 3)).reshape(B, T, H, -1)
    return o, s_fin


def deltanet_branch(dq, dk, dv, dz, d_alpha, d_beta, conv_w, a_log, dt_bias, norm_g, s_f0, s_b0):
    B, T, _ = dq.shape
    qkv = jax.nn.silu(dwconv3(jnp.concatenate([dq, dk, dv], axis=-1), conv_w))
    q, k, v = jnp.split(qkv, [DN_HEADS * DN_DK, 2 * DN_HEADS * DN_DK], axis=-1)
    q = l2_normalize(q.reshape(B, T, DN_HEADS, DN_DK)) * (DN_DK ** -0.5)
    k = l2_normalize(k.reshape(B, T, DN_HEADS, DN_DK))
    v = v.reshape(B, T, DN_HEADS, DN_DV).astype(jnp.float32)
    alpha = d_alpha.astype(jnp.float32).reshape(B, T, 2, DN_HEADS)
    beta = jax.nn.sigmoid(d_beta.astype(jnp.float32).reshape(B, T, 2, DN_HEADS))
    g = -jnp.exp(a_log.astype(jnp.float32)) * jax.nn.softplus(alpha + dt_bias.astype(jnp.float32))
    o_f, s_f = gated_delta_chunked(q, k, v, g[:, :, 0], beta[:, :, 0], s_f0)

    def flip(x):
        return jnp.flip(x, axis=1)

    o_b, s_b = gated_delta_chunked(flip(q), flip(k), flip(v), flip(g[:, :, 1]), flip(beta[:, :, 1]), s_b0)
    o = o_f + flip(o_b)
    z = dz.reshape(B, T, DN_HEADS, DN_DV).astype(jnp.float32)
    o = rms_norm(o, norm_g) * jax.nn.silu(z)
    return o.reshape(B, T, DN_HEADS * DN_DV).astype(dq.dtype), s_f.astype(dq.dtype), s_b.astype(dq.dtype)


def token_mixer(h, lw, rope, ctx_k, ctx_v, s_f0, s_b0):
    (w_in, conv_a_w, q_norm_g, k_norm_g, sink, dn_conv_w, dn_a_log, dn_dt_bias, dn_norm_g,
     w_out_a, w_out_b, w_out_c, w_o) = lw
    B, T, _ = h.shape
    (a_b, a_c, a_x, q, k, v, dq, dk, dv, dz, d_alpha, d_beta, gates) = jnp.split(h @ w_in, proj_split_points(), axis=-1)
    y_a = a_b * dwconv3(a_c * a_x, conv_a_w)
    q = rms_norm(q.reshape(B, T, H_ATT, HEAD_DIM), q_norm_g)
    k = rms_norm(k.reshape(B, T, KV_HEADS, HEAD_DIM), k_norm_g)
    v = v.reshape(B, T, KV_HEADS, HEAD_DIM)
    if ctx_k is None:
        y_b = context_attention(q, k, v, sink)
        s_f0 = jnp.zeros((B, DN_HEADS, DN_DK, DN_DV), jnp.float32)
        s_b0 = s_f0
    else:
        cos, sin = rope
        q = apply_axial_rope(q, cos, sin)
        k = apply_axial_rope(k, cos, sin)
        y_b = latent_window_attention(q, k, v, ctx_k, ctx_v, sink)
    y_c, s_f, s_b = deltanet_branch(dq, dk, dv, dz, d_alpha, d_beta, dn_conv_w, dn_a_log, dn_dt_bias,
                                    dn_norm_g, s_f0, s_b0)
    g_a, g_b, g_c = jnp.split(jax.nn.sigmoid(gates), 3, axis=-1)
    merged = g_a * (y_a @ w_out_a) + g_b * (y_b @ w_out_b) + g_c * (y_c @ w_out_c)
    return merged @ w_o, k, v, s_f, s_b


def swiglu(h, wg, wu, wd):
    return (jax.nn.silu(h @ wg) * (h @ wu)) @ wd


def moe_swiglu(h, w_r, b_r, wg, wu, wd):
    logits = jnp.einsum("btd,de->bte", h, w_r, preferred_element_type=jnp.float32) + b_r.astype(jnp.float32)
    top_v, top_i = lax.top_k(logits, TOP_K)
    combine = jnp.einsum("btk,btke->bte", jax.nn.softmax(top_v, axis=-1),
                         jax.nn.one_hot(top_i, N_EXPERTS, dtype=jnp.float32)).astype(h.dtype)
    out = jnp.zeros_like(h)
    for e in range(N_EXPERTS):
        out = out + combine[..., e:e + 1] * swiglu(h, wg[e], wu[e], wd[e])
    return out


def channel_mixer(h, layer, w_ff_gate, w_ff_up, w_ff_down, w_router, b_router, w_moe_gate, w_moe_up, w_moe_down):
    i = layer // 2
    if layer % 2 == 0:
        return swiglu(h, w_ff_gate[i], w_ff_up[i], w_ff_down[i])
    return moe_swiglu(h, w_router[i], b_router[i], w_moe_gate[i], w_moe_up[i], w_moe_down[i])


def setup_inputs(seed: int = 0) -> dict:
    key = jax.random.key(seed)
    ks = jax.random.split(key, 32)
    f32 = jnp.float32

    def nrm(i, shape, scale=1.0):
        return jax.random.normal(ks[i], shape, f32) * scale

    def gain(i, shape):
        return 1.0 + 0.02 * jax.random.normal(ks[i], shape, f32)

    dt = jnp.exp(jax.random.uniform(ks[20], (DEPTH, 2, DN_HEADS), f32) * (math.log(0.1) - math.log(0.001)) + math.log(0.001))
    return {
        "x_prompt": nrm(0, (BATCH, SEQ, D_MODEL)),
        "x_sample": nrm(1, (DEC_BATCH, DEC_SEQ, D_MODEL)),
        "cache_k": nrm(2, (DEC_BATCH, DEPTH, PAST_LEN, KV_HEADS, HEAD_DIM)),
        "cache_v": nrm(3, (DEC_BATCH, DEPTH, PAST_LEN, KV_HEADS, HEAD_DIM)),
        "state_fwd": nrm(4, (DEC_BATCH, DEPTH, DN_HEADS, DN_DK, DN_DV), 0.1),
        "state_bwd": nrm(5, (DEC_BATCH, DEPTH, DN_HEADS, DN_DK, DN_DV), 0.1),
        "c": nrm(6, (DEC_BATCH, D_MODEL)),
        "c_ctx": nrm(7, (D_MODEL,)),
        "norm1_g": gain(8, (DEPTH, D_MODEL)),
        "norm2_g": gain(9, (DEPTH, D_MODEL)),
        "w_ada": nrm(10, (DEPTH, D_MODEL, N_MOD * D_MODEL), 0.5 * D_MODEL ** -0.5),
        "b_ada": nrm(11, (DEPTH, N_MOD * D_MODEL), 0.01),
        "w_in": nrm(12, (DEPTH, D_MODEL, IN_WIDTH), D_MODEL ** -0.5),
        "conv_a_w": nrm(13, (DEPTH, 3, A_WIDTH), 3 ** -0.5),
        "q_norm_g": gain(14, (DEPTH, HEAD_DIM)),
        "k_norm_g": gain(15, (DEPTH, HEAD_DIM)),
        "attn_sink": nrm(16, (DEPTH, H_ATT), 0.5),
        "dn_conv_w": nrm(17, (DEPTH, 3, DN_CONV_WIDTH), 3 ** -0.5),
        "dn_a_log": jnp.log(jax.random.uniform(ks[18], (DEPTH, 2, DN_HEADS), f32, 1.0, 16.0)),
        "dn_dt_bias": dt + jnp.log(-jnp.expm1(-dt)),
        "dn_norm_g": gain(19, (DEPTH, DN_DV)),
        "w_out_a": nrm(21, (DEPTH, A_WIDTH, D_MODEL), A_WIDTH ** -0.5),
        "w_out_b": nrm(22, (DEPTH, H_ATT * HEAD_DIM, D_MODEL), (H_ATT * HEAD_DIM) ** -0.5),
        "w_out_c": nrm(23, (DEPTH, DN_HEADS * DN_DV, D_MODEL), (DN_HEADS * DN_DV) ** -0.5),
        "w_o": nrm(24, (DEPTH, D_MODEL, D_MODEL), D_MODEL ** -0.5),
        "w_ff_gate": nrm(25, (N_DENSE, D_MODEL, FF_DENSE), D_MODEL ** -0.5),
        "w_ff_up": nrm(26, (N_DENSE, D_MODEL, FF_DENSE), D_MODEL ** -0.5),
        "w_ff_down": nrm(27, (N_DENSE, FF_DENSE, D_MODEL), FF_DENSE ** -0.5),
        "w_router": nrm(28, (N_MOE, D_MODEL, N_EXPERTS), D_MODEL ** -0.5),
        "b_router": nrm(29, (N_MOE, N_EXPERTS), 0.01),
        "w_moe_gate": nrm(30, (N_MOE, N_EXPERTS, D_MODEL, FF_EXPERT), D_MODEL ** -0.5),
        "w_moe_up": nrm(31, (N_MOE, N_EXPERTS, D_MODEL, FF_EXPERT), D_MODEL ** -0.5),
        "w_moe_down": jax.random.normal(jax.random.fold_in(key, 99), (N_MOE, N_EXPERTS, FF_EXPERT, D_MODEL), f32) * FF_EXPERT ** -0.5,
    }


def reference(x_prompt, x_sample, cache_k, cache_v, state_fwd, state_bwd, c, c_ctx, norm1_g, norm2_g,
              w_ada, b_ada, w_in, conv_a_w, q_norm_g, k_norm_g, attn_sink, dn_conv_w, dn_a_log, dn_dt_bias,
              dn_norm_g, w_out_a, w_out_b, w_out_c, w_o, w_ff_gate, w_ff_up, w_ff_down, w_router, b_router,
              w_moe_gate, w_moe_up, w_moe_down):
    rope = axial_rope_tables(x_sample.shape[1])
    xp, xs = x_prompt, x_sample
    ks_out, vs_out, sf_out, sb_out = [], [], [], []
    for l in range(DEPTH):
        lw = (w_in[l], conv_a_w[l], q_norm_g[l], k_norm_g[l], attn_sink[l], dn_conv_w[l], dn_a_log[l],
              dn_dt_bias[l], dn_norm_g[l], w_out_a[l], w_out_b[l], w_out_c[l], w_o[l])
        sh1, sc1, g1, sh2, sc2, g2 = ada_modulation(c_ctx, w_ada[l], b_ada[l])
        out, k_l, v_l, sf_l, sb_l = token_mixer(modulate(xp, norm1_g[l], sh1, sc1), lw, None, None, None, None, None)
        xp = xp + g1 * out
        xp = xp + g2 * channel_mixer(modulate(xp, norm2_g[l], sh2, sc2), l, w_ff_gate, w_ff_up, w_ff_down,
                                     w_router, b_router, w_moe_gate, w_moe_up, w_moe_down)
        ks_out.append(k_l)
        vs_out.append(v_l)
        sf_out.append(sf_l)
        sb_out.append(sb_l)
        sh1, sc1, g1, sh2, sc2, g2 = ada_modulation(c, w_ada[l], b_ada[l])
        out = token_mixer(modulate(xs, norm1_g[l], sh1, sc1), lw, rope, cache_k[:, l], cache_v[:, l],
                          state_fwd[:, l], state_bwd[:, l])[0]
        xs = xs + g1 * out
        xs = xs + g2 * channel_mixer(modulate(xs, norm2_g[l], sh2, sc2), l, w_ff_gate, w_ff_up, w_ff_down,
                                     w_router, b_router, w_moe_gate, w_moe_up, w_moe_down)
    new_cache_k = jnp.stack(ks_out, axis=1)
    new_cache_v = jnp.stack(vs_out, axis=1)
    new_state_fwd = jnp.stack(sf_out, axis=1)
    new_state_bwd = jnp.stack(sb_out, axis=1)
    return (xp, xs, new_cache_k, new_cache_v, new_state_fwd, new_state_bwd)
```

```python
import functools

import jax
import jax.numpy as jnp
from jax import lax
from jax.experimental import pallas as pl
from jax.experimental.pallas import tpu as pltpu

F32 = jnp.float32
BF16 = jnp.bfloat16

D_MODEL = 1024
BATCH = 32
SEQ = 256
DEPTH = 4
DEC_BATCH = 4
DEC_SEQ = 4096
PAST_LEN = 256
GRID_W = 64
A_WIDTH = 512
HEAD_DIM = 64
H_ATT = 8
KV_HEADS = 2
ROPE_BASE = 10000.0
DN_HEADS = 4
DN_DK = 128
DN_DV = 128
DN_CHUNK = 64
FF_DENSE = 2816
N_EXPERTS = 8
FF_EXPERT = 1024
N_MOD = 6
EPS = 1e-6
NEG = -1e30

P_TOK = BATCH * SEQ
S_TOK = DEC_BATCH * DEC_SEQ
N_TOK = P_TOK + S_TOK
N_SEQ = BATCH + DEC_BATCH
LANE = 128
SUBLANE = 8

COL_A = 0
COL_DQKV = 1536
COL_Q = 3072
COL_DZ = 3584
COL_GATES = 4096
COL_KV = 7168
COL_AB = 7424
PROJ_W = 7680

VMEM_LIMIT = 56 * 1024 * 1024


def _cparams(*sem):
    return pltpu.CompilerParams(dimension_semantics=sem, vmem_limit_bytes=VMEM_LIMIT)


def _group_of_tile(i, tm):
    p_tiles = P_TOK // tm
    return jnp.where(i < p_tiles, 0, 1 + (i - p_tiles) // (DEC_SEQ // tm))


def _mod_spec(layer, k, tm):
    return pl.BlockSpec((None, None, None, 1, D_MODEL), lambda i, *_: (layer, _group_of_tile(i, tm), k, 0, 0))


def _seq_pos(i, tm):
    row = i * tm + lax.broadcasted_iota(jnp.int32, (tm, 1), 0)
    seqlen = jnp.where(i * tm < P_TOK, SEQ, DEC_SEQ)
    return row & (seqlen - 1), seqlen


def _halo_specs(tm, width, col_block):
    r = tm // SUBLANE
    last = N_TOK // SUBLANE - 1
    prev = pl.BlockSpec((SUBLANE, width), lambda i, *_: (jnp.maximum(i * r - 1, 0), col_block))
    nxt = pl.BlockSpec((SUBLANE, width), lambda i, *_: (jnp.minimum((i + 1) * r, last), col_block))
    return prev, nxt


def _dwconv3(x, prev_row, next_row, w, pos, seqlen):
    tm = x.shape[0]
    r = lax.broadcasted_iota(jnp.int32, (tm, 1), 0)
    xp = jnp.where(r == 0, prev_row, pltpu.roll(x, 1, 0))
    xn = jnp.where(r == tm - 1, next_row, pltpu.roll(x, tm - 1, 0))
    xp = jnp.where(pos == 0, 0.0, xp)
    xn = jnp.where(pos == seqlen - 1, 0.0, xn)
    return xp * w[0:1, :] + x * w[1:2, :] + xn * w[2:3, :]


def _silu(x):
    return x * jax.nn.sigmoid(x)


def _modulated_norm(x, g, shift, scale):
    y = x * lax.rsqrt(jnp.mean(x * x, axis=-1, keepdims=True) + EPS)
    return (y * g) * (1.0 + scale) + shift


def _mod_kernel(c_ref, w_ref, b_ref, o_ref):
    s = _silu(c_ref[...])
    o_ref[...] = jnp.dot(s.astype(BF16), w_ref[...].astype(BF16), preferred_element_type=F32) + b_ref[...]


def _modulations(cmat, w_ada, b_ada):
    tn = 1024
    out = pl.pallas_call(
        _mod_kernel,
        grid=(DEPTH, N_MOD * D_MODEL // tn),
        in_specs=[pl.BlockSpec((SUBLANE, D_MODEL), lambda l, j: (0, 0)),
                  pl.BlockSpec((None, D_MODEL, tn), lambda l, j: (l, 0, j)),
                  pl.BlockSpec((None, 1, tn), lambda l, j: (l, 0, j))],
        out_specs=pl.BlockSpec((None, SUBLANE, tn), lambda l, j: (l, 0, j)),
        out_shape=jax.ShapeDtypeStruct((DEPTH, SUBLANE, N_MOD * D_MODEL), F32),
        compiler_params=_cparams("arbitrary", "arbitrary"),
        name="ada_modulation",
    )(cmat, w_ada, b_ada.reshape(DEPTH, 1, N_MOD * D_MODEL))
    return out.reshape(DEPTH, SUBLANE, N_MOD, 1, D_MODEL)


def _proj_kernel(x_ref, sh_ref, sc_ref, g_ref, w_ref, o_ref, h_ref):
    @pl.when(pl.program_id(1) == 0)
    def _():
        h_ref[...] = _modulated_norm(x_ref[...], g_ref[...], sh_ref[...], sc_ref[...]).astype(BF16)

    o_ref[...] = jnp.dot(h_ref[...], w_ref[...], preferred_element_type=F32)


def _input_projection(x, mods, layer, g, w):
    tm, tn = 1024, 1536
    return pl.pallas_call(
        _proj_kernel,
        grid=(N_TOK // tm, PROJ_W // tn),
        in_specs=[pl.BlockSpec((tm, D_MODEL), lambda i, j: (i, 0)),
                  _mod_spec(layer, 0, tm), _mod_spec(layer, 1, tm),
                  pl.BlockSpec((1, D_MODEL), lambda i, j: (0, 0)),
                  pl.BlockSpec((D_MODEL, tn), lambda i, j: (0, j))],
        out_specs=pl.BlockSpec((tm, tn), lambda i, j: (i, j)),
        out_shape=jax.ShapeDtypeStruct((N_TOK, PROJ_W), F32),
        scratch_shapes=[pltpu.VMEM((tm, D_MODEL), BF16)],
        compiler_params=_cparams("arbitrary", "arbitrary"),
        name="input_projection",
    )(x, mods, mods, g, w)


def _pair_norm(x, gain):
    lane = lax.broadcasted_iota(jnp.int32, x.shape, 1)
    x2 = x * x
    lo = jnp.sum(jnp.where(lane < HEAD_DIM, x2, 0.0), axis=-1, keepdims=True)
    hi = jnp.sum(jnp.where(lane >= HEAD_DIM, x2, 0.0), axis=-1, keepdims=True)
    inv = jnp.where(lane < HEAD_DIM, lax.rsqrt(lo / HEAD_DIM + EPS), lax.rsqrt(hi / HEAD_DIM + EPS))
    return x * inv * gain


def _pair_rope(x, cos, sin):
    lane = lax.broadcasted_iota(jnp.int32, x.shape, 1)
    swapped = jnp.where((lane & 1) == 0, pltpu.roll(x, LANE - 1, 1), pltpu.roll(x, 1, 1))
    return x * cos + swapped * sin


def _lane_halves(x):
    lane = lax.broadcasted_iota(jnp.int32, x.shape, 1)
    xr = pltpu.roll(x, HEAD_DIM, 1)
    lo, hi = lane < HEAD_DIM, lane >= HEAD_DIM
    return (jnp.where(lo, x, 0.0), jnp.where(hi, xr, 0.0), jnp.where(lo, xr, 0.0), jnp.where(hi, x, 0.0))


def _qk_prep_kernel(q_ref, kv_ref, cos_ref, sin_ref, qg_ref, kg_ref, qo_ref, ko_ref, vo_ref, kn_ref, v_ref):
    cos, sin = cos_ref[...], sin_ref[...]
    for p in range(H_ATT // 2):
        qn = _pair_norm(q_ref[:, p * LANE:(p + 1) * LANE], qg_ref[...])
        qo_ref[:, p * LANE:(p + 1) * LANE] = (_pair_rope(qn, cos, sin) * (HEAD_DIM ** -0.5)).astype(BF16)
    k = _pair_norm(kv_ref[:, 0:LANE], kg_ref[...])
    v = kv_ref[:, LANE:2 * LANE]
    kn_ref[...] = k
    v_ref[...] = v
    for j, (kx, vx) in enumerate(zip(_lane_halves(_pair_rope(k, cos, sin)), _lane_halves(v))):
        ko_ref[:, j * LANE:(j + 1) * LANE] = kx.astype(BF16)
        vo_ref[:, j * LANE:(j + 1) * LANE] = vx.astype(BF16)


def _qk_prep(proj, cos_t, sin_t, qg, kg):
    tm = 256
    tab = lambda i: jnp.where(i < P_TOK // tm, 0, 1 + (i - P_TOK // tm) % (DEC_SEQ // tm))
    wide = jax.ShapeDtypeStruct((N_TOK, 4 * LANE), BF16)
    slab = jax.ShapeDtypeStruct((N_TOK, LANE), F32)
    return pl.pallas_call(
        _qk_prep_kernel,
        grid=(N_TOK // tm,),
        in_specs=[pl.BlockSpec((tm, 4 * LANE), lambda i: (i, COL_Q // (4 * LANE))),
                  pl.BlockSpec((tm, 2 * LANE), lambda i: (i, COL_KV // (2 * LANE))),
                  pl.BlockSpec((tm, LANE), lambda i: (tab(i), 0)),
                  pl.BlockSpec((tm, LANE), lambda i: (tab(i), 0)),
                  pl.BlockSpec((1, LANE), lambda i: (0, 0)),
                  pl.BlockSpec((1, LANE), lambda i: (0, 0))],
        out_specs=[pl.BlockSpec((tm, 4 * LANE), lambda i: (i, 0)),
                   pl.BlockSpec((tm, 4 * LANE), lambda i: (i, 0)),
                   pl.BlockSpec((tm, 4 * LANE), lambda i: (i, 0)),
                   pl.BlockSpec((tm, LANE), lambda i: (i, 0)),
                   pl.BlockSpec((tm, LANE), lambda i: (i, 0))],
        out_shape=[wide, wide, wide, slab, slab],
        compiler_params=_cparams("arbitrary"),
        name="qk_prep",
    )(proj, proj, cos_t, sin_t, qg, kg)


def _rope_tables():
    pos = jnp.arange(DEC_SEQ)
    row = (pos // GRID_W).astype(F32)
    col = (pos % GRID_W).astype(F32)
    n_freq = HEAD_DIM // 4
    inv_freq = ROPE_BASE ** (-jnp.arange(n_freq, dtype=F32) / n_freq)
    ang = jnp.concatenate([row[:, None] * inv_freq, col[:, None] * inv_freq], axis=-1)
    cos = jnp.repeat(jnp.cos(ang), 2, axis=-1)
    sin = jnp.stack([-jnp.sin(ang), jnp.sin(ang)], axis=-1).reshape(DEC_SEQ, HEAD_DIM)
    cos = jnp.concatenate([jnp.ones((SEQ, HEAD_DIM), F32), cos], axis=0)
    sin = jnp.concatenate([jnp.zeros((SEQ, HEAD_DIM), F32), sin], axis=0)
    return jnp.tile(cos, (1, 2)), jnp.tile(sin, (1, 2))


def _attend_kv_head(q2, kx, vx, sink_a, sink_b, mask, kvh):
    nq = q2.shape[0] // 2
    row = lax.broadcasted_iota(jnp.int32, (2 * nq, 1), 0)
    out = None
    for half in range(2):
        j = 2 * kvh + half
        s = lax.dot_general(q2, kx[:, j * LANE:(j + 1) * LANE], (((1,), (1,)), ((), ())), preferred_element_type=F32)
        if mask is not None:
            s = jnp.where(mask, s, NEG)
        sink = jnp.where(row < nq, sink_a[half], sink_b[half])
        m = jnp.maximum(jnp.max(s, axis=-1, keepdims=True), sink)
        e = jnp.exp(s - m)
        denom = jnp.sum(e, axis=-1, keepdims=True) + jnp.exp(sink - m)
        o = jnp.dot(e.astype(BF16), vx[:, j * LANE:(j + 1) * LANE], preferred_element_type=F32) / denom
        out = o if out is None else out + o
    return out


def _ctx_attn_kernel(sink_ref, q_ref, k_ref, v_ref, o_ref):
    kx, vx = k_ref[...], v_ref[...]
    for kvh in range(KV_HEADS):
        p0, p1 = 2 * kvh, 2 * kvh + 1
        q2 = jnp.concatenate([q_ref[:, p0 * LANE:(p0 + 1) * LANE], q_ref[:, p1 * LANE:(p1 + 1) * LANE]], axis=0)
        sa = (sink_ref[2 * p0], sink_ref[2 * p0 + 1])
        sb = (sink_ref[2 * p1], sink_ref[2 * p1 + 1])
        o = _attend_kv_head(q2, kx, vx, sa, sb, None, kvh)
        o_ref[:, p0 * LANE:(p0 + 1) * LANE] = o[:SEQ]
        o_ref[:, p1 * LANE:(p1 + 1) * LANE] = o[SEQ:]


def _context_attention(sink, qs, kx, vx):
    blk = lambda: pl.BlockSpec((SEQ, 4 * LANE), lambda b: (b, 0))
    return pl.pallas_call(
        _ctx_attn_kernel,
        grid=(BATCH,),
        in_specs=[pl.BlockSpec(memory_space=pltpu.SMEM), blk(), blk(), blk()],
        out_specs=blk(),
        out_shape=jax.ShapeDtypeStruct((P_TOK, 4 * LANE), F32),
        compiler_params=_cparams("arbitrary"),
        name="context_attention",
    )(sink, qs, kx, vx)


LAT_BLOCK = 128


def _lat_attn_kernel(sink_ref, q_ref, kp_ref, kc_ref, kn_ref, vp_ref, vc_ref, vn_ref, ck_ref, cv_ref, o_ref):
    n = pl.program_id(1)
    nb = pl.num_programs(1)
    ck = jnp.concatenate(_lane_halves(ck_ref[...]), axis=1).astype(BF16)
    cv = jnp.concatenate(_lane_halves(cv_ref[...]), axis=1).astype(BF16)
    kx = jnp.concatenate([kp_ref[...], kc_ref[...], kn_ref[...], ck], axis=0)
    vx = jnp.concatenate([vp_ref[...], vc_ref[...], vn_ref[...], cv], axis=0)
    n_keys = 3 * LAT_BLOCK + PAST_LEN
    qi = lax.broadcasted_iota(jnp.int32, (2 * LAT_BLOCK, n_keys), 0) & (LAT_BLOCK - 1)
    kj = lax.broadcasted_iota(jnp.int32, (2 * LAT_BLOCK, n_keys), 1)
    far = 2 * LAT_BLOCK
    off_prev = jnp.where(n > 0, 0, far)
    off_next = jnp.where(n < nb - 1, 0, far)
    t_prev = jnp.where(kj < LAT_BLOCK, kj - qi - off_prev, 0)
    t_next = jnp.where(kj >= 2 * LAT_BLOCK, jnp.where(kj < 3 * LAT_BLOCK, qi - off_next - (kj - 2 * LAT_BLOCK), 0), 0)
    mask = jnp.minimum(t_prev, t_next) >= 0
    for kvh in range(KV_HEADS):
        p0, p1 = 2 * kvh, 2 * kvh + 1
        q2 = jnp.concatenate([q_ref[:, p0 * LANE:(p0 + 1) * LANE], q_ref[:, p1 * LANE:(p1 + 1) * LANE]], axis=0)
        sa = (sink_ref[2 * p0], sink_ref[2 * p0 + 1])
        sb = (sink_ref[2 * p1], sink_ref[2 * p1 + 1])
        o = _attend_kv_head(q2, kx, vx, sa, sb, mask, kvh)
        o_ref[:, p0 * LANE:(p0 + 1) * LANE] = o[:LAT_BLOCK]
        o_ref[:, p1 * LANE:(p1 + 1) * LANE] = o[LAT_BLOCK:]


def _latent_attention(sink, qs, kx, vx, cache_k, cache_v, layer):
    nb = DEC_SEQ // LAT_BLOCK
    base = P_TOK // LAT_BLOCK
    cur = lambda b, n: (base + b * nb + n, 0)
    prev = lambda b, n: (base + b * nb + jnp.maximum(n - 1, 0), 0)
    nxt = lambda b, n: (base + b * nb + jnp.minimum(n + 1, nb - 1), 0)
    blk = lambda f: pl.BlockSpec((LAT_BLOCK, 4 * LANE), f)
    ctx = pl.BlockSpec((None, None, PAST_LEN, LANE), lambda b, n: (b, layer, 0, 0))
    return pl.pallas_call(
        _lat_attn_kernel,
        grid=(DEC_BATCH, nb),
        in_specs=[pl.BlockSpec(memory_space=pltpu.SMEM), blk(cur),
                  blk(prev), blk(cur), blk(nxt), blk(prev), blk(cur), blk(nxt), ctx, ctx],
        out_specs=pl.BlockSpec((LAT_BLOCK, 4 * LANE), lambda b, n: (b * nb + n, 0)),
        out_shape=jax.ShapeDtypeStruct((S_TOK, 4 * LANE), F32),
        compiler_params=_cparams("arbitrary", "arbitrary"),
        name="latent_attention",
    )(sink, qs, kx, kx, kx, vx, vx, vx, cache_k, cache_v)


DN_W = DN_HEADS * DN_DK
N_GB = 2 * 2 * DN_HEADS


def _dn_prep_kernel(x_ref, xp_ref, xn_ref, ab_ref, w_ref, alog_ref, dtb_ref, q_ref, k_ref, v_ref, gb_ref, gbt_ref):
    tm = x_ref.shape[0]
    pos, seqlen = _seq_pos(pl.program_id(0), tm)
    c = _dwconv3(x_ref[...], xp_ref[SUBLANE - 1:SUBLANE, :], xn_ref[0:1, :], w_ref[...], pos, seqlen)
    s = _silu(c)
    for h in range(DN_HEADS):
        sl = slice(h * DN_DK, (h + 1) * DN_DK)
        qh = s[:, sl]
        kh = s[:, DN_W + h * DN_DK:DN_W + (h + 1) * DN_DK]
        q_ref[:, sl] = qh * lax.rsqrt(jnp.sum(qh * qh, axis=-1, keepdims=True) + EPS) * (DN_DK ** -0.5)
        k_ref[:, sl] = kh * lax.rsqrt(jnp.sum(kh * kh, axis=-1, keepdims=True) + EPS)
    v_ref[...] = s[:, 2 * DN_W:]
    ab = ab_ref[...]
    lane = lax.broadcasted_iota(jnp.int32, ab.shape, 1)
    z = ab + dtb_ref[...]
    softplus = jnp.maximum(z, 0.0) + jnp.log1p(jnp.exp(-jnp.abs(z)))
    gb = jnp.where(lane < N_GB // 2, -jnp.exp(alog_ref[...]) * softplus, jax.nn.sigmoid(ab))
    gb_ref[...] = gb
    gbt = gb.T
    for ch in range(tm // DN_CHUNK):
        gbt_ref[ch] = gbt[0:N_GB, ch * DN_CHUNK:(ch + 1) * DN_CHUNK]


def _dn_prep(proj, conv_w, alog, dtb):
    tm = 256
    wq = 3 * DN_W
    prev, nxt = _halo_specs(tm, wq, COL_DQKV // wq)
    tok = jax.ShapeDtypeStruct((N_TOK, DN_W), F32)
    return pl.pallas_call(
        _dn_prep_kernel,
        grid=(N_TOK // tm,),
        in_specs=[pl.BlockSpec((tm, wq), lambda i: (i, COL_DQKV // wq)), prev, nxt,
                  pl.BlockSpec((tm, LANE), lambda i: (i, COL_AB // LANE)),
                  pl.BlockSpec((3, wq), lambda i: (0, 0)),
                  pl.BlockSpec((1, LANE), lambda i: (0, 0)),
                  pl.BlockSpec((1, LANE), lambda i: (0, 0))],
        out_specs=[pl.BlockSpec((tm, DN_W), lambda i: (i, 0)),
                   pl.BlockSpec((tm, DN_W), lambda i: (i, 0)),
                   pl.BlockSpec((tm, DN_W), lambda i: (i, 0)),
                   pl.BlockSpec((tm, LANE), lambda i: (i, 0)),
                   pl.BlockSpec((tm // DN_CHUNK, N_GB, DN_CHUNK), lambda i: (i, 0, 0))],
        out_shape=[tok, tok, tok, jax.ShapeDtypeStruct((N_TOK, LANE), F32),
                   jax.ShapeDtypeStruct((N_TOK // DN_CHUNK, N_GB, DN_CHUNK), F32)],
        compiler_params=_cparams("arbitrary"),
        name="deltanet_prep",
    )(proj, proj, proj, proj, conv_w, alog, dtb)


def _dot_hi(a, b):
    return jnp.dot(a, b, precision=lax.Precision.HIGHEST, preferred_element_type=F32)


def _dot_nt(a, b):
    return lax.dot_general(a.astype(BF16), b.astype(BF16), (((1,), (1,)), ((), ())), preferred_element_type=F32)


def _dot_tn(a, b):
    return lax.dot_general(a.astype(BF16), b.astype(BF16), (((0,), (0,)), ((), ())), preferred_element_type=F32)


def _dot_bf(a, b):
    return jnp.dot(a.astype(BF16), b.astype(BF16), preferred_element_type=F32)


def _unit_triangular_inverse(lm):
    c = lm.shape[0]
    eye = (lax.broadcasted_iota(jnp.int32, (c, c), 0) == lax.broadcasted_iota(jnp.int32, (c, c), 1)).astype(F32)
    x = -lm
    p = eye + x
    span = 2
    while span < c:
        x = _dot_hi(x, x)
        p = p + _dot_hi(p, x)
        span *= 2
    return p


def _scan_step_indices(t):
    pc, sc = SEQ // DN_CHUNK, DEC_SEQ // DN_CHUNK
    p_steps = BATCH * pc
    is_p = t < p_steps
    ts = t - p_steps
    seq = jnp.where(is_p, t // pc, BATCH + ts // sc)
    s = jnp.where(is_p, t % pc, ts % sc)
    n = jnp.where(is_p, pc, sc)
    base = jnp.where(is_p, (t // pc) * pc, P_TOK // DN_CHUNK + (ts // sc) * sc)
    return seq, s, n, base


def _dn_scan_kernel(qf_ref, kf_ref, vf_ref, gf_ref, gtf_ref, qb_ref, kb_ref, vb_ref, gb_ref, gtb_ref,
                    s0f_ref, s0b_ref, of_ref, ob_ref, sf_ref, sb_ref, state_ref):
    _, s, n, _ = _scan_step_indices(pl.program_id(0))
    c = DN_CHUNK
    ri = lax.broadcasted_iota(jnp.int32, (c, c), 0)
    ci = lax.broadcasted_iota(jnp.int32, (c, c), 1)

    @pl.when(s == 0)
    def _():
        state_ref[0:DN_HEADS] = s0f_ref[...]
        state_ref[DN_HEADS:2 * DN_HEADS] = s0b_ref[...]

    dirs = ((qf_ref, kf_ref, vf_ref, gf_ref, gtf_ref, of_ref, ri >= ci, ri > ci, c - 1),
            (qb_ref, kb_ref, vb_ref, gb_ref, gtb_ref, ob_ref, ri <= ci, ri < ci, 0))
    for d, (q_ref, k_ref, v_ref, g_ref, gt_ref, o_ref, incl, strict, last) in enumerate(dirs):
        cum = incl.astype(F32)
        gb = g_ref[...]
        gcs = _dot_hi(cum, gb)
        gct = lax.dot_general(gt_ref[...], cum, (((1,), (1,)), ((), ())), precision=lax.Precision.HIGHEST,
                              preferred_element_type=F32)
        for h in range(DN_HEADS):
            col = d * DN_HEADS + h
            sl = slice(h * DN_DK, (h + 1) * DN_DK)
            gcol = gcs[:, col:col + 1]
            grow = gct[col:col + 1, :]
            beta = gb[:, N_GB // 2 + col:N_GB // 2 + col + 1]
            decay = jnp.exp(jnp.where(incl, gcol - grow, NEG))
            qh, kh, vh = q_ref[:, sl], k_ref[:, sl], v_ref[:, sl]
            kbeta = kh * beta
            tri = _unit_triangular_inverse(jnp.where(strict, _dot_nt(kbeta, kh) * decay, 0.0))
            eg = jnp.exp(gcol)
            uw = _dot_hi(tri, jnp.concatenate([vh * beta, kbeta * eg], axis=1))
            u, w = uw[:, :DN_DV], uw[:, DN_DV:]
            qk = jnp.where(incl, _dot_nt(qh, kh) * decay, 0.0)
            glast = gcol[last:last + 1, :]
            state = state_ref[col]
            v_new = u - _dot_bf(w, state)
            o_ref[:, sl] = _dot_bf(qh * eg, state) + _dot_bf(qk, v_new)
            state_ref[col] = state * jnp.exp(glast) + _dot_tn(kh * jnp.exp(glast - gcol), v_new)

    @pl.when(s == n - 1)
    def _():
        sf_ref[...] = state_ref[0:DN_HEADS]
        sb_ref[...] = state_ref[DN_HEADS:2 * DN_HEADS]


def _dn_scan(qn, kn, vs, gb, gbt, s0f, s0b):
    steps = BATCH * (SEQ // DN_CHUNK) + DEC_BATCH * (DEC_SEQ // DN_CHUNK)

    def fwd(t):
        _, s, _, base = _scan_step_indices(t)
        return base + s

    def bwd(t):
        _, s, n, base = _scan_step_indices(t)
        return base + n - 1 - s

    def seq_of(t):
        return _scan_step_indices(t)[0]

    def specs(f):
        tok = lambda: pl.BlockSpec((DN_CHUNK, DN_W), lambda t: (f(t), 0))
        return [tok(), tok(), tok(), pl.BlockSpec((DN_CHUNK, LANE), lambda t: (f(t), 0)),
                pl.BlockSpec((None, N_GB, DN_CHUNK), lambda t: (f(t), 0, 0))]

    st = lambda: pl.BlockSpec((None, DN_HEADS, DN_DK, DN_DV), lambda t: (seq_of(t), 0, 0, 0))
    tok_out = jax.ShapeDtypeStruct((N_TOK, DN_W), F32)
    st_out = jax.ShapeDtypeStruct((N_SEQ, DN_HEADS, DN_DK, DN_DV), F32)
    return pl.pallas_call(
        _dn_scan_kernel,
        grid=(steps,),
        in_specs=specs(fwd) + specs(bwd) + [st(), st()],
        out_specs=[pl.BlockSpec((DN_CHUNK, DN_W), lambda t: (fwd(t), 0)),
                   pl.BlockSpec((DN_CHUNK, DN_W), lambda t: (bwd(t), 0)), st(), st()],
        out_shape=[tok_out, tok_out, st_out, st_out],
        scratch_shapes=[pltpu.VMEM((2 * DN_HEADS, DN_DK, DN_DV), F32)],
        compiler_params=_cparams("arbitrary"),
        name="deltanet_scan",
    )(qn, kn, vs, gb, gbt, qn, kn, vs, gb, gbt, s0f, s0b)


def _merge_kernel(a_ref, ap_ref, an_ref, yb_ref, of_ref, ob_ref, z_ref, ga_ref, gb_ref, gc_ref, x_ref, g1_ref,
                  cw_ref, ng_ref, wa_ref, wb_ref, wc_ref, wo_ref, o_ref):
    tm = x_ref.shape[0]
    pos, seqlen = _seq_pos(pl.program_id(0), tm)
    w = A_WIDTH
    u = a_ref[:, w:2 * w] * a_ref[:, 2 * w:3 * w]
    up = ap_ref[SUBLANE - 1:SUBLANE, w:2 * w] * ap_ref[SUBLANE - 1:SUBLANE, 2 * w:3 * w]
    un = an_ref[0:1, w:2 * w] * an_ref[0:1, 2 * w:3 * w]
    y_a = a_ref[:, 0:w] * _dwconv3(u, up, un, cw_ref[...], pos, seqlen)
    merged = jax.nn.sigmoid(ga_ref[...]) * _dot_bf(y_a, wa_ref[...])
    merged += jax.nn.sigmoid(gb_ref[...]) * _dot_bf(yb_ref[...], wb_ref[...])
    parts = []
    for h in range(DN_HEADS):
        sl = slice(h * DN_DV, (h + 1) * DN_DV)
        o = of_ref[:, sl] + ob_ref[:, sl]
        o = o * lax.rsqrt(jnp.mean(o * o, axis=-1, keepdims=True) + EPS) * ng_ref[...]
        parts.append(o * _silu(z_ref[:, sl]))
    y_c = jnp.concatenate(parts, axis=1)
    merged += jax.nn.sigmoid(gc_ref[...]) * _dot_bf(y_c, wc_ref[...])
    o_ref[...] = x_ref[...] + g1_ref[...] * _dot_bf(merged, wo_ref[...])


def _merge(proj, y_b, o_f, o_b, x, mods, layer, conv_a_w, dn_norm_g, wa, wb, wc, wo):
    tm = 256
    wa3 = 3 * A_WIDTH
    prev, nxt = _halo_specs(tm, wa3, COL_A // wa3)
    tok = lambda w, cb=0: pl.BlockSpec((tm, w), lambda i: (i, cb))
    full = lambda a: pl.BlockSpec(a.shape, lambda i: (0,) * a.ndim)
    gate0 = COL_GATES // D_MODEL
    return pl.pallas_call(
        _merge_kernel,
        grid=(N_TOK // tm,),
        in_specs=[tok(wa3, COL_A // wa3), prev, nxt, tok(A_WIDTH), tok(DN_W), tok(DN_W), tok(DN_W, COL_DZ // DN_W),
                  tok(D_MODEL, gate0), tok(D_MODEL, gate0 + 1), tok(D_MODEL, gate0 + 2), tok(D_MODEL),
                  _mod_spec(layer, 2, tm), full(conv_a_w), full(dn_norm_g), full(wa), full(wb), full(wc), full(wo)],
        out_specs=tok(D_MODEL),
        out_shape=jax.ShapeDtypeStruct((N_TOK, D_MODEL), F32),
        compiler_params=_cparams("arbitrary"),
        name="mixer_merge",
    )(proj, proj, proj, y_b, o_f, o_b, proj, proj, proj, proj, x, mods, conv_a_w, dn_norm_g, wa, wb, wc, wo)


def _ffn_kernel(x_ref, sh_ref, sc_ref, g2_ref, g_ref, wg_ref, wu_ref, wd_ref, o_ref, h_ref, acc_ref):
    f = pl.program_id(1)

    @pl.when(f == 0)
    def _():
        h_ref[...] = _modulated_norm(x_ref[...], g_ref[...], sh_ref[...], sc_ref[...]).astype(BF16)
        acc_ref[...] = jnp.zeros_like(acc_ref)

    h = h_ref[...]
    a = _silu(jnp.dot(h, wg_ref[...], preferred_element_type=F32)) * jnp.dot(h, wu_ref[...], preferred_element_type=F32)
    acc_ref[...] += jnp.dot(a.astype(BF16), wd_ref[...], preferred_element_type=F32)

    @pl.when(f == pl.num_programs(1) - 1)
    def _():
        o_ref[...] = x_ref[...] + g2_ref[...] * acc_ref[...]


def _dense_ffn(x, mods, layer, g, wg, wu, wd):
    tm, tf = 512, 1408
    return pl.pallas_call(
        _ffn_kernel,
        grid=(N_TOK // tm, FF_DENSE // tf),
        in_specs=[pl.BlockSpec((tm, D_MODEL), lambda i, f: (i, 0)),
                  _mod_spec(layer, 3, tm), _mod_spec(layer, 4, tm), _mod_spec(layer, 5, tm),
                  pl.BlockSpec((1, D_MODEL), lambda i, f: (0, 0)),
                  pl.BlockSpec((D_MODEL, tf), lambda i, f: (0, f)),
                  pl.BlockSpec((D_MODEL, tf), lambda i, f: (0, f)),
                  pl.BlockSpec((tf, D_MODEL), lambda i, f: (f, 0))],
        out_specs=pl.BlockSpec((tm, D_MODEL), lambda i, f: (i, 0)),
        out_shape=jax.ShapeDtypeStruct((N_TOK, D_MODEL), F32),
        scratch_shapes=[pltpu.VMEM((tm, D_MODEL), BF16), pltpu.VMEM((tm, D_MODEL), F32)],
        compiler_params=_cparams("arbitrary", "arbitrary"),
        name="dense_ffn",
    )(x, mods, mods, mods, g, wg, wu, wd)


def _top2_combine(logits):
    lane = lax.broadcasted_iota(jnp.int32, logits.shape, 1)
    m1 = jnp.max(logits, axis=-1, keepdims=True)
    i1 = jnp.min(jnp.where(logits == m1, lane, LANE), axis=-1, keepdims=True)
    rest = jnp.where(lane == i1, -jnp.inf, logits)
    m2 = jnp.max(rest, axis=-1, keepdims=True)
    i2 = jnp.min(jnp.where(rest == m2, lane, LANE), axis=-1, keepdims=True)
    e2 = jnp.exp(m2 - m1)
    return jnp.where(lane == i1, 1.0 / (1.0 + e2), 0.0) + jnp.where(lane == i2, e2 / (1.0 + e2), 0.0)


def _moe_kernel(x_ref, sh_ref, sc_ref, g2_ref, g_ref, wr_ref, br_ref, wg_ref, wu_ref, wd_ref, o_ref,
                h_ref, comb_ref, acc_ref):
    e = pl.program_id(1)

    @pl.when(e == 0)
    def _():
        h = _modulated_norm(x_ref[...], g_ref[...], sh_ref[...], sc_ref[...])
        h_ref[...] = h.astype(BF16)
        logits = jnp.dot(h.astype(BF16), wr_ref[...].astype(BF16), preferred_element_type=F32) + br_ref[...]
        comb_ref[...] = _top2_combine(logits)
        acc_ref[...] = jnp.zeros_like(acc_ref)

    h = h_ref[...]
    lane = lax.broadcasted_iota(jnp.int32, comb_ref.shape, 1)
    ce = jnp.sum(jnp.where(lane == e, comb_ref[...], 0.0), axis=-1, keepdims=True)
    a = _silu(jnp.dot(h, wg_ref[...], preferred_element_type=F32)) * jnp.dot(h, wu_ref[...], preferred_element_type=F32)
    acc_ref[...] += ce * jnp.dot(a.astype(BF16), wd_ref[...], preferred_element_type=F32)

    @pl.when(e == pl.num_programs(1) - 1)
    def _():
        o_ref[...] = x_ref[...] + g2_ref[...] * acc_ref[...]


def _moe_ffn(x, mods, layer, g, wr, br, wg, wu, wd):
    tm = 512
    ew = lambda: pl.BlockSpec((None, D_MODEL, FF_EXPERT), lambda i, e: (e, 0, 0))
    return pl.pallas_call(
        _moe_kernel,
        grid=(N_TOK // tm, N_EXPERTS),
        in_specs=[pl.BlockSpec((tm, D_MODEL), lambda i, e: (i, 0)),
                  _mod_spec(layer, 3, tm), _mod_spec(layer, 4, tm), _mod_spec(layer, 5, tm),
                  pl.BlockSpec((1, D_MODEL), lambda i, e: (0, 0)),
                  pl.BlockSpec((D_MODEL, LANE), lambda i, e: (0, 0)),
                  pl.BlockSpec((1, LANE), lambda i, e: (0, 0)),
                  ew(), ew(), ew()],
        out_specs=pl.BlockSpec((tm, D_MODEL), lambda i, e: (i, 0)),
        out_shape=jax.ShapeDtypeStruct((N_TOK, D_MODEL), F32),
        scratch_shapes=[pltpu.VMEM((tm, D_MODEL), BF16), pltpu.VMEM((tm, LANE), F32), pltpu.VMEM((tm, D_MODEL), F32)],
        compiler_params=_cparams("arbitrary", "arbitrary"),
        name="moe_ffn",
    )(x, mods, mods, mods, g, wr, br, wg, wu, wd)


def _reorder_w_in(w):
    pad = jnp.zeros((w.shape[0], PROJ_W - (COL_AB + N_GB)), w.dtype)
    return jnp.concatenate([w[:, 0:1536], w[:, 2304:3840], w[:, 1536:2048], w[:, 3840:4352], w[:, 4368:7440],
                            w[:, 2048:2304], w[:, 4352:4368], pad], axis=1).astype(BF16)


def _pad_lanes(v, fill=0.0):
    v = v.reshape(1, -1)
    return jnp.concatenate([v, jnp.full((1, LANE - v.shape[1]), fill, v.dtype)], axis=1)


def kernel(x_prompt, x_sample, cache_k, cache_v, state_fwd, state_bwd, c, c_ctx, norm1_g, norm2_g, w_ada, b_ada, w_in, conv_a_w, q_norm_g, k_norm_g, attn_sink, dn_conv_w, dn_a_log, dn_dt_bias, dn_norm_g, w_out_a, w_out_b, w_out_c, w_o, w_ff_gate, w_ff_up, w_ff_down, w_router, b_router, w_moe_gate, w_moe_up, w_moe_down):
    x = jnp.concatenate([x_prompt.reshape(P_TOK, D_MODEL), x_sample.reshape(S_TOK, D_MODEL)], axis=0)
    cmat = jnp.concatenate([c_ctx[None, :], c, jnp.zeros((SUBLANE - 1 - DEC_BATCH, D_MODEL), F32)], axis=0)
    mods = _modulations(cmat, w_ada, b_ada)
    cos_t, sin_t = _rope_tables()
    ck = cache_k.reshape(DEC_BATCH, DEPTH, PAST_LEN, LANE)
    cv = cache_v.reshape(DEC_BATCH, DEPTH, PAST_LEN, LANE)
    zero_state = jnp.zeros((BATCH, DN_HEADS, DN_DK, DN_DV), F32)

    ks_out, vs_out, sf_out, sb_out = [], [], [], []
    for l in range(DEPTH):
        proj = _input_projection(x, mods, l, norm1_g[l][None, :], _reorder_w_in(w_in[l]))

        qs, kx, vx, k_norm, v_raw = _qk_prep(proj, cos_t, sin_t, jnp.tile(q_norm_g[l], 2)[None, :],
                                             jnp.tile(k_norm_g[l], 2)[None, :])
        y_b = jnp.concatenate([_context_attention(attn_sink[l], qs, kx, vx),
                               _latent_attention(attn_sink[l], qs, kx, vx, ck, cv, l)], axis=0)
        ks_out.append(k_norm[:P_TOK].reshape(BATCH, SEQ, KV_HEADS, HEAD_DIM))
        vs_out.append(v_raw[:P_TOK].reshape(BATCH, SEQ, KV_HEADS, HEAD_DIM))

        qn, kn, vs, gb, gbt = _dn_prep(proj, dn_conv_w[l], _pad_lanes(dn_a_log[l]), _pad_lanes(dn_dt_bias[l]))
        s0f = jnp.concatenate([zero_state, state_fwd[:, l]], axis=0)
        s0b = jnp.concatenate([zero_state, state_bwd[:, l]], axis=0)
        o_f, o_b, s_f, s_b = _dn_scan(qn, kn, vs, gb, gbt, s0f, s0b)
        sf_out.append(s_f[:BATCH])
        sb_out.append(s_b[:BATCH])

        x = _merge(proj, y_b, o_f, o_b, x, mods, l, conv_a_w[l], dn_norm_g[l][None, :],
                   w_out_a[l].astype(BF16), w_out_b[l].astype(BF16), w_out_c[l].astype(BF16), w_o[l].astype(BF16))

        i = l // 2
        if l % 2 == 0:
            x = _dense_ffn(x, mods, l, norm2_g[l][None, :], w_ff_gate[i].astype(BF16), w_ff_up[i].astype(BF16),
                           w_ff_down[i].astype(BF16))
        else:
            wr = jnp.concatenate([w_router[i], jnp.zeros((D_MODEL, LANE - N_EXPERTS), F32)], axis=1)
            x = _moe_ffn(x, mods, l, norm2_g[l][None, :], wr, _pad_lanes(b_router[i], -jnp.inf),
                         w_moe_gate[i].astype(BF16), w_moe_up[i].astype(BF16), w_moe_down[i].astype(BF16))

    return (x[:P_TOK].reshape(BATCH, SEQ, D_MODEL), x[P_TOK:].reshape(DEC_BATCH, DEC_SEQ, D_MODEL),
            jnp.stack(ks_out, axis=1), jnp.stack(vs_out, axis=1), jnp.stack(sf_out, axis=1), jnp.stack(sb_out, axis=1))
```

```python
import functools

import jax
import jax.numpy as jnp
from jax import lax
from jax.experimental import pallas as pl
from jax.experimental.pallas import tpu as pltpu

F32 = jnp.float32
BF16 = jnp.bfloat16

D_MODEL = 1024
BATCH = 32
SEQ = 256
DEPTH = 4
DEC_BATCH = 4
DEC_SEQ = 4096
PAST_LEN = 256
GRID_W = 64
A_WIDTH = 512
HEAD_DIM = 64
H_ATT = 8
KV_HEADS = 2
ROPE_BASE = 10000.0
DN_HEADS = 4
DN_DK = 128
DN_DV = 128
DN_CHUNK = 64
FF_DENSE = 2816
N_EXPERTS = 8
FF_EXPERT = 1024
N_MOD = 6
EPS = 1e-6
NEG = -1e30

P_TOK = BATCH * SEQ
S_TOK = DEC_BATCH * DEC_SEQ
N_TOK = P_TOK + S_TOK
N_SEQ = BATCH + DEC_BATCH
LANE = 128
SUBLANE = 8

COL_A = 0
COL_DQKV = 1536
COL_Q = 3072
COL_DZ = 3584
COL_GATES = 4096
COL_KV = 7168
COL_AB = 7424
PROJ_W = 7680

VMEM_LIMIT = 56 * 1024 * 1024


def _cparams(*sem):
    return pltpu.CompilerParams(dimension_semantics=sem, vmem_limit_bytes=VMEM_LIMIT)


def _group_of_tile(i, tm):
    p_tiles = P_TOK // tm
    return jnp.where(i < p_tiles, 0, 1 + (i - p_tiles) // (DEC_SEQ // tm))


def _mod_spec(layer, k, tm):
    return pl.BlockSpec((None, None, None, 1, D_MODEL), lambda i, *_: (layer, _group_of_tile(i, tm), k, 0, 0))


def _seq_pos(i, tm):
    row = i * tm + lax.broadcasted_iota(jnp.int32, (tm, 1), 0)
    seqlen = jnp.where(i * tm < P_TOK, SEQ, DEC_SEQ)
    return row & (seqlen - 1), seqlen


def _halo_specs(tm, width, col_block):
    r = tm // SUBLANE
    last = N_TOK // SUBLANE - 1
    prev = pl.BlockSpec((SUBLANE, width), lambda i, *_: (jnp.maximum(i * r - 1, 0), col_block))
    nxt = pl.BlockSpec((SUBLANE, width), lambda i, *_: (jnp.minimum((i + 1) * r, last), col_block))
    return prev, nxt


def _dwconv3(x, prev_row, next_row, w, pos, seqlen):
    tm = x.shape[0]
    r = lax.broadcasted_iota(jnp.int32, (tm, 1), 0)
    xp = jnp.where(r == 0, prev_row, pltpu.roll(x, 1, 0))
    xn = jnp.where(r == tm - 1, next_row, pltpu.roll(x, tm - 1, 0))
    xp = jnp.where(pos == 0, 0.0, xp)
    xn = jnp.where(pos == seqlen - 1, 0.0, xn)
    return xp * w[0:1, :] + x * w[1:2, :] + xn * w[2:3, :]


def _silu(x):
    return x * jax.nn.sigmoid(x)


def _modulated_norm(x, g, shift, scale):
    y = x * lax.rsqrt(jnp.mean(x * x, axis=-1, keepdims=True) + EPS)
    return (y * g) * (1.0 + scale) + shift


def _mod_kernel(c_ref, w_ref, b_ref, o_ref):
    s = _silu(c_ref[...])
    o_ref[...] = jnp.dot(s.astype(BF16), w_ref[...].astype(BF16), preferred_element_type=F32) + b_ref[...]


def _modulations(cmat, w_ada, b_ada):
    tn = 1024
    out = pl.pallas_call(
        _mod_kernel,
        grid=(DEPTH, N_MOD * D_MODEL // tn),
        in_specs=[pl.BlockSpec((SUBLANE, D_MODEL), lambda l, j: (0, 0)),
                  pl.BlockSpec((None, D_MODEL, tn), lambda l, j: (l, 0, j)),
                  pl.BlockSpec((None, 1, tn), lambda l, j: (l, 0, j))],
        out_specs=pl.BlockSpec((None, SUBLANE, tn), lambda l, j: (l, 0, j)),
        out_shape=jax.ShapeDtypeStruct((DEPTH, SUBLANE, N_MOD * D_MODEL), F32),
        compiler_params=_cparams("arbitrary", "arbitrary"),
        name="ada_modulation",
    )(cmat, w_ada, b_ada.reshape(DEPTH, 1, N_MOD * D_MODEL))
    return out.reshape(DEPTH, SUBLANE, N_MOD, 1, D_MODEL)


def _proj_kernel(x_ref, sh_ref, sc_ref, g_ref, w_ref, o_ref, h_ref):
    @pl.when(pl.program_id(1) == 0)
    def _():
        h_ref[...] = _modulated_norm(x_ref[...], g_ref[...], sh_ref[...], sc_ref[...]).astype(BF16)

    o_ref[...] = jnp.dot(h_ref[...], w_ref[...], preferred_element_type=F32)


def _input_projection(x, mods, layer, g, w):
    tm, tn = 1024, 1536
    return pl.pallas_call(
        _proj_kernel,
        grid=(N_TOK // tm, PROJ_W // tn),
        in_specs=[pl.BlockSpec((tm, D_MODEL), lambda i, j: (i, 0)),
                  _mod_spec(layer, 0, tm), _mod_spec(layer, 1, tm),
                  pl.BlockSpec((1, D_MODEL), lambda i, j: (0, 0)),
                  pl.BlockSpec((D_MODEL, tn), lambda i, j: (0, j))],
        out_specs=pl.BlockSpec((tm, tn), lambda i, j: (i, j)),
        out_shape=jax.ShapeDtypeStruct((N_TOK, PROJ_W), F32),
        scratch_shapes=[pltpu.VMEM((tm, D_MODEL), BF16)],
        compiler_params=_cparams("arbitrary", "arbitrary"),
        name="input_projection",
    )(x, mods, mods, g, w)


def _pair_norm(x, gain):
    lane = lax.broadcasted_iota(jnp.int32, x.shape, 1)
    x2 = x * x
    lo = jnp.sum(jnp.where(lane < HEAD_DIM, x2, 0.0), axis=-1, keepdims=True)
    hi = jnp.sum(jnp.where(lane >= HEAD_DIM, x2, 0.0), axis=-1, keepdims=True)
    inv = jnp.where(lane < HEAD_DIM, lax.rsqrt(lo / HEAD_DIM + EPS), lax.rsqrt(hi / HEAD_DIM + EPS))
    return x * inv * gain


def _pair_rope(x, cos, sin):
    lane = lax.broadcasted_iota(jnp.int32, x.shape, 1)
    swapped = jnp.where((lane & 1) == 0, pltpu.roll(x, LANE - 1, 1), pltpu.roll(x, 1, 1))
    return x * cos + swapped * sin


def _lane_halves(x):
    lane = lax.broadcasted_iota(jnp.int32, x.shape, 1)
    xr = pltpu.roll(x, HEAD_DIM, 1)
    lo, hi = lane < HEAD_DIM, lane >= HEAD_DIM
    return (jnp.where(lo, x, 0.0), jnp.where(hi, xr, 0.0), jnp.where(lo, xr, 0.0), jnp.where(hi, x, 0.0))


def _qk_prep_kernel(q_ref, kv_ref, cos_ref, sin_ref, qg_ref, kg_ref, qo_ref, ko_ref, vo_ref, kn_ref, v_ref):
    cos, sin = cos_ref[...], sin_ref[...]
    for p in range(H_ATT // 2):
        qn = _pair_norm(q_ref[:, p * LANE:(p + 1) * LANE], qg_ref[...])
        qo_ref[:, p * LANE:(p + 1) * LANE] = (_pair_rope(qn, cos, sin) * (HEAD_DIM ** -0.5)).astype(BF16)
    k = _pair_norm(kv_ref[:, 0:LANE], kg_ref[...])
    v = kv_ref[:, LANE:2 * LANE]
    kn_ref[...] = k
    v_ref[...] = v
    for j, (kx, vx) in enumerate(zip(_lane_halves(_pair_rope(k, cos, sin)), _lane_halves(v))):
        ko_ref[:, j * LANE:(j + 1) * LANE] = kx.astype(BF16)
        vo_ref[:, j * LANE:(j + 1) * LANE] = vx.astype(BF16)


def _qk_prep(proj, cos_t, sin_t, qg, kg):
    tm = 256
    tab = lambda i: jnp.where(i < P_TOK // tm, 0, 1 + (i - P_TOK // tm) % (DEC_SEQ // tm))
    wide = jax.ShapeDtypeStruct((N_TOK, 4 * LANE), BF16)
    slab = jax.ShapeDtypeStruct((N_TOK, LANE), F32)
    return pl.pallas_call(
        _qk_prep_kernel,
        grid=(N_TOK // tm,),
        in_specs=[pl.BlockSpec((tm, 4 * LANE), lambda i: (i, COL_Q // (4 * LANE))),
                  pl.BlockSpec((tm, 2 * LANE), lambda i: (i, COL_KV // (2 * LANE))),
                  pl.BlockSpec((tm, LANE), lambda i: (tab(i), 0)),
                  pl.BlockSpec((tm, LANE), lambda i: (tab(i), 0)),
                  pl.BlockSpec((1, LANE), lambda i: (0, 0)),
                  pl.BlockSpec((1, LANE), lambda i: (0, 0))],
        out_specs=[pl.BlockSpec((tm, 4 * LANE), lambda i: (i, 0)),
                   pl.BlockSpec((tm, 4 * LANE), lambda i: (i, 0)),
                   pl.BlockSpec((tm, 4 * LANE), lambda i: (i, 0)),
                   pl.BlockSpec((tm, LANE), lambda i: (i, 0)),
                   pl.BlockSpec((tm, LANE), lambda i: (i, 0))],
        out_shape=[wide, wide, wide, slab, slab],
        compiler_params=_cparams("arbitrary"),
        name="qk_prep",
    )(proj, proj, cos_t, sin_t, qg, kg)


def _rope_tables():
    pos = jnp.arange(DEC_SEQ)
    row = (pos // GRID_W).astype(F32)
    col = (pos % GRID_W).astype(F32)
    n_freq = HEAD_DIM // 4
    inv_freq = ROPE_BASE ** (-jnp.arange(n_freq, dtype=F32) / n_freq)
    ang = jnp.concatenate([row[:, None] * inv_freq, col[:, None] * inv_freq], axis=-1)
    cos = jnp.repeat(jnp.cos(ang), 2, axis=-1)
    sin = jnp.stack([-jnp.sin(ang), jnp.sin(ang)], axis=-1).reshape(DEC_SEQ, HEAD_DIM)
    cos = jnp.concatenate([jnp.ones((SEQ, HEAD_DIM), F32), cos], axis=0)
    sin = jnp.concatenate([jnp.zeros((SEQ, HEAD_DIM), F32), sin], axis=0)
    return jnp.tile(cos, (1, 2)), jnp.tile(sin, (1, 2))


def _attend_kv_head(q2, kx, vx, sink_a, sink_b, mask, kvh):
    nq = q2.shape[0] // 2
    row = lax.broadcasted_iota(jnp.int32, (2 * nq, 1), 0)
    out = None
    for half in range(2):
        j = 2 * kvh + half
        s = lax.dot_general(q2, kx[:, j * LANE:(j + 1) * LANE], (((1,), (1,)), ((), ())), preferred_element_type=F32)
        if mask is not None:
            s = jnp.where(mask, s, NEG)
        sink = jnp.where(row < nq, sink_a[half], sink_b[half])
        m = jnp.maximum(jnp.max(s, axis=-1, keepdims=True), sink)
        e = jnp.exp(s - m)
        denom = jnp.sum(e, axis=-1, keepdims=True) + jnp.exp(sink - m)
        o = jnp.dot(e.astype(BF16), vx[:, j * LANE:(j + 1) * LANE], preferred_element_type=F32) / denom
        out = o if out is None else out + o
    return out


def _ctx_attn_kernel(sink_ref, q_ref, k_ref, v_ref, o_ref):
    kx, vx = k_ref[...], v_ref[...]
    for kvh in range(KV_HEADS):
        p0, p1 = 2 * kvh, 2 * kvh + 1
        q2 = jnp.concatenate([q_ref[:, p0 * LANE:(p0 + 1) * LANE], q_ref[:, p1 * LANE:(p1 + 1) * LANE]], axis=0)
        sa = (sink_ref[2 * p0], sink_ref[2 * p0 + 1])
        sb = (sink_ref[2 * p1], sink_ref[2 * p1 + 1])
        o = _attend_kv_head(q2, kx, vx, sa, sb, None, kvh)
        o_ref[:, p0 * LANE:(p0 + 1) * LANE] = o[:SEQ]
        o_ref[:, p1 * LANE:(p1 + 1) * LANE] = o[SEQ:]


def _context_attention(sink, qs, kx, vx):
    blk = lambda: pl.BlockSpec((SEQ, 4 * LANE), lambda b: (b, 0))
    return pl.pallas_call(
        _ctx_attn_kernel,
        grid=(BATCH,),
        in_specs=[pl.BlockSpec(memory_space=pltpu.SMEM), blk(), blk(), blk()],
        out_specs=blk(),
        out_shape=jax.ShapeDtypeStruct((P_TOK, 4 * LANE), F32),
        compiler_params=_cparams("arbitrary"),
        name="context_attention",
    )(sink, qs, kx, vx)


LAT_BLOCK = 128


def _lat_attn_kernel(sink_ref, q_ref, kp_ref, kc_ref, kn_ref, vp_ref, vc_ref, vn_ref, ck_ref, cv_ref, o_ref):
    n = pl.program_id(1)
    nb = pl.num_programs(1)
    ck = jnp.concatenate(_lane_halves(ck_ref[...]), axis=1).astype(BF16)
    cv = jnp.concatenate(_lane_halves(cv_ref[...]), axis=1).astype(BF16)
    kx = jnp.concatenate([kp_ref[...], kc_ref[...], kn_ref[...], ck], axis=0)
    vx = jnp.concatenate([vp_ref[...], vc_ref[...], vn_ref[...], cv], axis=0)
    n_keys = 3 * LAT_BLOCK + PAST_LEN
    qi = lax.broadcasted_iota(jnp.int32, (2 * LAT_BLOCK, n_keys), 0) & (LAT_BLOCK - 1)
    kj = lax.broadcasted_iota(jnp.int32, (2 * LAT_BLOCK, n_keys), 1)
    far = 2 * LAT_BLOCK
    off_prev = jnp.where(n > 0, 0, far)
    off_next = jnp.where(n < nb - 1, 0, far)
    t_prev = jnp.where(kj < LAT_BLOCK, kj - qi - off_prev, 0)
    t_next = jnp.where(kj >= 2 * LAT_BLOCK, jnp.where(kj < 3 * LAT_BLOCK, qi - off_next - (kj - 2 * LAT_BLOCK), 0), 0)
    mask = jnp.minimum(t_prev, t_next) >= 0
    for kvh in range(KV_HEADS):
        p0, p1 = 2 * kvh, 2 * kvh + 1
        q2 = jnp.concatenate([q_ref[:, p0 * LANE:(p0 + 1) * LANE], q_ref[:, p1 * LANE:(p1 + 1) * LANE]], axis=0)
        sa = (sink_ref[2 * p0], sink_ref[2 * p0 + 1])
        sb = (sink_ref[2 * p1], sink_ref[2 * p1 + 1])
        o = _attend_kv_head(q2, kx, vx, sa, sb, mask, kvh)
        o_ref[:, p0 * LANE:(p0 + 1) * LANE] = o[:LAT_BLOCK]
        o_ref[:, p1 * LANE:(p1 + 1) * LANE] = o[LAT_BLOCK:]


def _latent_attention(sink, qs, kx, vx, cache_k, cache_v, layer):
    nb = DEC_SEQ // LAT_BLOCK
    base = P_TOK // LAT_BLOCK
    cur = lambda b, n: (base + b * nb + n, 0)
    prev = lambda b, n: (base + b * nb + jnp.maximum(n - 1, 0), 0)
    nxt = lambda b, n: (base + b * nb + jnp.minimum(n + 1, nb - 1), 0)
    blk = lambda f: pl.BlockSpec((LAT_BLOCK, 4 * LANE), f)
    ctx = pl.BlockSpec((None, None, PAST_LEN, LANE), lambda b, n: (b, layer, 0, 0))
    return pl.pallas_call(
        _lat_attn_kernel,
        grid=(DEC_BATCH, nb),
        in_specs=[pl.BlockSpec(memory_space=pltpu.SMEM), blk(cur),
                  blk(prev), blk(cur), blk(nxt), blk(prev), blk(cur), blk(nxt), ctx, ctx],
        out_specs=pl.BlockSpec((LAT_BLOCK, 4 * LANE), lambda b, n: (b * nb + n, 0)),
        out_shape=jax.ShapeDtypeStruct((S_TOK, 4 * LANE), F32),
        compiler_params=_cparams("arbitrary", "arbitrary"),
        name="latent_attention",
    )(sink, qs, kx, kx, kx, vx, vx, vx, cache_k, cache_v)


DN_W = DN_HEADS * DN_DK
N_GB = 2 * 2 * DN_HEADS
DN_ROWS = DN_HEADS * DN_CHUNK
DN_TILE = SEQ
CH_PER_TILE = DN_TILE // DN_CHUNK
SEQ_GROUP = DEC_BATCH
P_GC = (BATCH // SEQ_GROUP) * (SEQ // DN_CHUNK)
N_GC = P_GC + DEC_SEQ // DN_CHUNK


def _dot_hi(a, b):
    return jnp.dot(a, b, precision=lax.Precision.HIGHEST, preferred_element_type=F32)


def _dot_nt(a, b):
    return lax.dot_general(a.astype(BF16), b.astype(BF16), (((1,), (1,)), ((), ())), preferred_element_type=F32)


def _dot_tn(a, b):
    return lax.dot_general(a.astype(BF16), b.astype(BF16), (((0,), (0,)), ((), ())), preferred_element_type=F32)


def _dot_bf(a, b):
    return jnp.dot(a.astype(BF16), b.astype(BF16), preferred_element_type=F32)


def _stack_heads(x):
    return jnp.concatenate([x[:, h * DN_DK:(h + 1) * DN_DK] for h in range(DN_HEADS)], axis=0)


def _stack_cols(m, c0):
    return jnp.concatenate([m[:, c0 + h:c0 + h + 1] for h in range(DN_HEADS)], axis=0)


def _head(x, h):
    return x[h * DN_CHUNK:(h + 1) * DN_CHUNK]


def _per_head(f):
    return jnp.concatenate([f(h) for h in range(DN_HEADS)], axis=0)


def _tri_inverses(lms, xor):
    eye = (xor == 0).astype(F32)
    ts = [eye - jnp.where(xor < 2, lm, 0.0) for lm in lms]
    k = 1
    while (2 << k) <= DN_CHUNK:
        level = (xor >> k) == 1
        ms = [jnp.where(level, lm, 0.0).astype(BF16) for lm in lms]
        tbs = [t.astype(BF16) for t in ts]
        mts = [_per_head(lambda h: _dot_bf(_head(m, h), _head(tb, h))).astype(BF16) for m, tb in zip(ms, tbs)]
        ts = [t - _per_head(lambda h: _dot_bf(_head(tb, h), _head(mt, h))) for t, tb, mt in zip(ts, tbs, mts)]
        k += 1
    return ts


def _dn_tile_slot(i):
    p_tiles = P_TOK // DN_TILE
    tiles_per_seq = DEC_SEQ // DN_TILE
    j = i - p_tiles
    blk = jnp.where(i < p_tiles, i // SEQ_GROUP, P_GC // CH_PER_TILE + j % tiles_per_seq)
    q = jnp.where(i < p_tiles, i % SEQ_GROUP, j // tiles_per_seq)
    return blk, q


def _dn_local_kernel(x_ref, xp_ref, xn_ref, ab_ref, w_ref, alog_ref, dtb_ref,
                     u_ref, w_out_ref, qd_ref, kd_ref, qk_ref, egl_ref):
    pos, seqlen = _seq_pos(pl.program_id(0), DN_TILE)
    s = _silu(_dwconv3(x_ref[...], xp_ref[SUBLANE - 1:SUBLANE, :], xn_ref[0:1, :], w_ref[...], pos, seqlen))
    ab = ab_ref[...]
    lane = lax.broadcasted_iota(jnp.int32, ab.shape, 1)
    z = ab + dtb_ref[...]
    softplus = jnp.maximum(z, 0.0) + jnp.log1p(jnp.exp(-jnp.abs(z)))
    gb = jnp.where(lane < N_GB // 2, -jnp.exp(alog_ref[...]) * softplus, jax.nn.sigmoid(ab))

    c, r = DN_CHUNK, DN_ROWS
    ri = lax.broadcasted_iota(jnp.int32, (r, c), 0) & (c - 1)
    ci = lax.broadcasted_iota(jnp.int32, (r, c), 1)
    xor = ri ^ ci
    masks = ((ri >= ci, ri > ci), (ri <= ci, ri < ci))
    tri = (lax.broadcasted_iota(jnp.int32, (c, c), 0) >= lax.broadcasted_iota(jnp.int32, (c, c), 1)).astype(F32)

    problems, lms = [], []
    for ch in range(CH_PER_TILE):
        sc = s[ch * c:(ch + 1) * c]
        qst = _stack_heads(sc[:, 0:DN_W])
        kst = _stack_heads(sc[:, DN_W:2 * DN_W])
        vst = _stack_heads(sc[:, 2 * DN_W:3 * DN_W])
        qst = qst * lax.rsqrt(jnp.sum(qst * qst, axis=-1, keepdims=True) + EPS) * (DN_DK ** -0.5)
        kst = kst * lax.rsqrt(jnp.sum(kst * kst, axis=-1, keepdims=True) + EPS)
        gbc = gb[ch * c:(ch + 1) * c]
        prefix = _dot_hi(tri, gbc)
        suffix = prefix[c - 1:c] - prefix + gbc
        qb, kb = qst.astype(BF16), kst.astype(BF16)
        gram = _per_head(lambda h: _dot_nt(_head(kb, h), _head(kb, h)))
        qk0 = _per_head(lambda h: _dot_nt(_head(qb, h), _head(kb, h)))
        for d, (gc, last) in enumerate(((prefix, c - 1), (suffix, 0))):
            incl, strict = masks[d]
            c0 = d * DN_HEADS
            gcol = _stack_cols(gc, c0)
            beta = _stack_cols(gbc, N_GB // 2 + c0)
            gct = gc.T
            grow = _per_head(lambda h: jnp.broadcast_to(gct[c0 + h:c0 + h + 1, :], (c, c)))
            decay = jnp.exp(jnp.where(incl, gcol - grow, NEG))
            lms.append(jnp.where(strict, gram * beta * decay, 0.0))
            eg = jnp.exp(gcol)
            glast = _per_head(lambda h: jnp.broadcast_to(gc[last:last + 1, c0 + h:c0 + h + 1], (c, 1)))
            qd_ref[ch, d] = (qst * eg).astype(BF16)
            kd_ref[ch, d] = (kst * jnp.exp(glast - gcol)).astype(BF16)
            qk_ref[ch, d] = jnp.where(incl, qk0 * decay, 0.0).astype(BF16)
            egl_ref[ch, d] = jnp.concatenate([jnp.broadcast_to(jnp.exp(gc[last:last + 1, c0 + h:c0 + h + 1]), (1, DN_DV))
                                              for h in range(DN_HEADS)], axis=1)
            problems.append((ch, d, jnp.concatenate([vst * beta, kst * (beta * eg)], axis=1).astype(BF16)))

    for (ch, d, rhs), t in zip(problems, _tri_inverses(lms, xor)):
        tb = t.astype(BF16)
        x = _per_head(lambda h: jnp.dot(_head(tb, h), _head(rhs, h), preferred_element_type=F32))
        u_ref[ch, d] = x[:, 0:DN_DV]
        w_out_ref[ch, d] = x[:, DN_DV:2 * DN_DV].astype(BF16)


def _dn_local(proj, conv_w, alog, dtb):
    wq = 3 * DN_W
    prev, nxt = _halo_specs(DN_TILE, wq, COL_DQKV // wq)

    def slot_spec(*tail):
        return pl.BlockSpec((CH_PER_TILE, None, 2) + tail,
                            lambda i: _dn_tile_slot(i) + (0,) * (1 + len(tail)))

    def slot_shape(dtype, *tail):
        return jax.ShapeDtypeStruct((N_GC, SEQ_GROUP, 2) + tail, dtype)

    return pl.pallas_call(
        _dn_local_kernel,
        grid=(N_TOK // DN_TILE,),
        in_specs=[pl.BlockSpec((DN_TILE, wq), lambda i: (i, COL_DQKV // wq)), prev, nxt,
                  pl.BlockSpec((DN_TILE, LANE), lambda i: (i, COL_AB // LANE)),
                  pl.BlockSpec((3, wq), lambda i: (0, 0)),
                  pl.BlockSpec((1, LANE), lambda i: (0, 0)),
                  pl.BlockSpec((1, LANE), lambda i: (0, 0))],
        out_specs=[slot_spec(DN_ROWS, DN_DV), slot_spec(DN_ROWS, DN_DK), slot_spec(DN_ROWS, DN_DK),
                   slot_spec(DN_ROWS, DN_DK), slot_spec(DN_ROWS, DN_CHUNK), slot_spec(1, DN_W)],
        out_shape=[slot_shape(F32, DN_ROWS, DN_DV), slot_shape(BF16, DN_ROWS, DN_DK), slot_shape(BF16, DN_ROWS, DN_DK),
                   slot_shape(BF16, DN_ROWS, DN_DK), slot_shape(BF16, DN_ROWS, DN_CHUNK), slot_shape(F32, 1, DN_W)],
        compiler_params=_cparams("arbitrary"),
        name="deltanet_local",
    )(proj, proj, proj, proj, conv_w, alog, dtb)


def _scan_step(t):
    pc, sc = SEQ // DN_CHUNK, DEC_SEQ // DN_CHUNK
    is_p = t < P_GC
    g = jnp.where(is_p, t // pc, BATCH // SEQ_GROUP)
    s = jnp.where(is_p, t % pc, t - P_GC)
    n = jnp.where(is_p, pc, sc)
    return g, s, n, t, t - s + (n - 1 - s)


def _dn_scan_kernel(uf_ref, wf_ref, qdf_ref, kdf_ref, qkf_ref, egf_ref, ub_ref, wb_ref, qdb_ref, kdb_ref, qkb_ref,
                    egb_ref, s0f_ref, s0b_ref, of_ref, ob_ref, sf_ref, sb_ref, state_ref):
    _, s, n, _, _ = _scan_step(pl.program_id(0))
    c, r = DN_CHUNK, DN_ROWS
    heads = range(DN_HEADS)

    @pl.when(s == 0)
    def _():
        for q in range(SEQ_GROUP):
            state_ref[0, q] = jnp.concatenate([s0f_ref[q, h] for h in heads], axis=1)
            state_ref[1, q] = jnp.concatenate([s0b_ref[q, h] for h in heads], axis=1)

    head_slab = (lax.broadcasted_iota(jnp.int32, (r, DN_W), 0) // c) == (lax.broadcasted_iota(jnp.int32, (r, DN_W), 1) // DN_DV)
    dirs = ((uf_ref, wf_ref, qdf_ref, kdf_ref, qkf_ref, egf_ref, of_ref),
            (ub_ref, wb_ref, qdb_ref, kdb_ref, qkb_ref, egb_ref, ob_ref))
    chains = [(d, q) + refs for d, refs in enumerate(dirs) for q in range(SEQ_GROUP)]
    states = [state_ref[d, q] for d, q, *_ in chains]
    boths = [jnp.dot(jnp.concatenate([w_ref[q], qd_ref[q]], axis=0), st.astype(BF16), preferred_element_type=F32)
             for (d, q, u_ref, w_ref, qd_ref, *_), st in zip(chains, states)]
    v_news = [(u_ref[q] - jnp.concatenate([b[h * c:(h + 1) * c, h * DN_DV:(h + 1) * DN_DV] for h in heads], axis=0)
               ).astype(BF16) for (d, q, u_ref, *_), b in zip(chains, boths)]
    for (d, q, u_ref, w_ref, qd_ref, kd_ref, qk_ref, eg_ref, o_ref), st, b, v_new in zip(chains, states, boths, v_news):
        qs = jnp.concatenate([b[r + h * c:r + (h + 1) * c, h * DN_DV:(h + 1) * DN_DV] for h in heads], axis=0)
        qk = qk_ref[q]
        o = qs + _per_head(lambda h: jnp.dot(_head(qk, h), _head(v_new, h), preferred_element_type=F32))
        v_wide = jnp.concatenate([v_new] * DN_HEADS, axis=1)
        v_wide = jnp.where(head_slab, v_wide, jnp.zeros_like(v_wide))
        state_ref[d, q] = st * eg_ref[q] + lax.dot_general(kd_ref[q], v_wide, (((0,), (0,)), ((), ())),
                                                           preferred_element_type=F32)
        o_ref[q] = jnp.concatenate([o[h * c:(h + 1) * c] for h in heads], axis=1)

    @pl.when(s == n - 1)
    def _():
        for q in range(SEQ_GROUP):
            for h in heads:
                sf_ref[q, h] = state_ref[0, q][:, h * DN_DV:(h + 1) * DN_DV]
                sb_ref[q, h] = state_ref[1, q][:, h * DN_DV:(h + 1) * DN_DV]


def _dn_scan(u, w, qd, kd, qk, egl, s0f, s0b):
    fwd = lambda t: _scan_step(t)[3]
    bwd = lambda t: _scan_step(t)[4]
    grp = lambda t: _scan_step(t)[0]

    def specs(slot, d):
        sp = lambda *tail: pl.BlockSpec((None, SEQ_GROUP, None) + tail, lambda t: (slot(t), 0, d) + (0,) * len(tail))
        return [sp(DN_ROWS, DN_DV), sp(DN_ROWS, DN_DK), sp(DN_ROWS, DN_DK), sp(DN_ROWS, DN_DK), sp(DN_ROWS, DN_CHUNK),
                sp(1, DN_W)]

    st = lambda: pl.BlockSpec((SEQ_GROUP, DN_HEADS, DN_DK, DN_DV), lambda t: (grp(t), 0, 0, 0))
    o_spec = lambda slot: pl.BlockSpec((None, SEQ_GROUP, DN_CHUNK, DN_W), lambda t: (slot(t), 0, 0, 0))
    o_shape = jax.ShapeDtypeStruct((N_GC, SEQ_GROUP, DN_CHUNK, DN_W), F32)
    st_shape = jax.ShapeDtypeStruct((N_SEQ, DN_HEADS, DN_DK, DN_DV), F32)
    args = (u, w, qd, kd, qk, egl)
    return pl.pallas_call(
        _dn_scan_kernel,
        grid=(N_GC,),
        in_specs=specs(fwd, 0) + specs(bwd, 1) + [st(), st()],
        out_specs=[o_spec(fwd), o_spec(bwd), st(), st()],
        out_shape=[o_shape, o_shape, st_shape, st_shape],
        scratch_shapes=[pltpu.VMEM((2, SEQ_GROUP, DN_DK, DN_W), F32)],
        compiler_params=_cparams("arbitrary"),
        name="deltanet_scan",
    )(*args, *args, s0f, s0b)


def _merge_kernel(a_ref, ap_ref, an_ref, yb_ref, of_ref, ob_ref, z_ref, ga_ref, gb_ref, gc_ref, x_ref, g1_ref,
                  cw_ref, ng_ref, wa_ref, wb_ref, wc_ref, wo_ref, o_ref):
    tm = x_ref.shape[0]
    pos, seqlen = _seq_pos(pl.program_id(0), tm)
    w = A_WIDTH
    u = a_ref[:, w:2 * w] * a_ref[:, 2 * w:3 * w]
    up = ap_ref[SUBLANE - 1:SUBLANE, w:2 * w] * ap_ref[SUBLANE - 1:SUBLANE, 2 * w:3 * w]
    un = an_ref[0:1, w:2 * w] * an_ref[0:1, 2 * w:3 * w]
    y_a = a_ref[:, 0:w] * _dwconv3(u, up, un, cw_ref[...], pos, seqlen)
    merged = jax.nn.sigmoid(ga_ref[...]) * _dot_bf(y_a, wa_ref[...])
    merged += jax.nn.sigmoid(gb_ref[...]) * _dot_bf(yb_ref[...], wb_ref[...])
    o_sum = (of_ref[...] + ob_ref[...]).reshape(tm, DN_W)
    parts = []
    for h in range(DN_HEADS):
        sl = slice(h * DN_DV, (h + 1) * DN_DV)
        o = o_sum[:, sl]
        o = o * lax.rsqrt(jnp.mean(o * o, axis=-1, keepdims=True) + EPS) * ng_ref[...]
        parts.append(o * _silu(z_ref[:, sl]))
    y_c = jnp.concatenate(parts, axis=1)
    merged += jax.nn.sigmoid(gc_ref[...]) * _dot_bf(y_c, wc_ref[...])
    o_ref[...] = x_ref[...] + g1_ref[...] * _dot_bf(merged, wo_ref[...])


def _merge(proj, y_b, o_f, o_b, x, mods, layer, conv_a_w, dn_norm_g, wa, wb, wc, wo):
    tm = DN_TILE
    wa3 = 3 * A_WIDTH
    prev, nxt = _halo_specs(tm, wa3, COL_A // wa3)
    tok = lambda w, cb=0: pl.BlockSpec((tm, w), lambda i: (i, cb))
    full = lambda a: pl.BlockSpec(a.shape, lambda i: (0,) * a.ndim)
    scan_out = lambda: pl.BlockSpec((CH_PER_TILE, None, DN_CHUNK, DN_W), lambda i: _dn_tile_slot(i) + (0, 0))
    gate0 = COL_GATES // D_MODEL
    return pl.pallas_call(
        _merge_kernel,
        grid=(N_TOK // tm,),
        in_specs=[tok(wa3, COL_A // wa3), prev, nxt, tok(A_WIDTH), scan_out(), scan_out(), tok(DN_W, COL_DZ // DN_W),
                  tok(D_MODEL, gate0), tok(D_MODEL, gate0 + 1), tok(D_MODEL, gate0 + 2), tok(D_MODEL),
                  _mod_spec(layer, 2, tm), full(conv_a_w), full(dn_norm_g), full(wa), full(wb), full(wc), full(wo)],
        out_specs=tok(D_MODEL),
        out_shape=jax.ShapeDtypeStruct((N_TOK, D_MODEL), F32),
        compiler_params=_cparams("arbitrary"),
        name="mixer_merge",
    )(proj, proj, proj, y_b, o_f, o_b, proj, proj, proj, proj, x, mods, conv_a_w, dn_norm_g, wa, wb, wc, wo)


def _ffn_kernel(x_ref, sh_ref, sc_ref, g2_ref, g_ref, wg_ref, wu_ref, wd_ref, o_ref, h_ref, acc_ref):
    f = pl.program_id(1)

    @pl.when(f == 0)
    def _():
        h_ref[...] = _modulated_norm(x_ref[...], g_ref[...], sh_ref[...], sc_ref[...]).astype(BF16)
        acc_ref[...] = jnp.zeros_like(acc_ref)

    h = h_ref[...]
    a = _silu(jnp.dot(h, wg_ref[...], preferred_element_type=F32)) * jnp.dot(h, wu_ref[...], preferred_element_type=F32)
    acc_ref[...] += jnp.dot(a.astype(BF16), wd_ref[...], preferred_element_type=F32)

    @pl.when(f == pl.num_programs(1) - 1)
    def _():
        o_ref[...] = x_ref[...] + g2_ref[...] * acc_ref[...]


def _dense_ffn(x, mods, layer, g, wg, wu, wd):
    tm, tf = 512, 1408
    return pl.pallas_call(
        _ffn_kernel,
        grid=(N_TOK // tm, FF_DENSE // tf),
        in_specs=[pl.BlockSpec((tm, D_MODEL), lambda i, f: (i, 0)),
                  _mod_spec(layer, 3, tm), _mod_spec(layer, 4, tm), _mod_spec(layer, 5, tm),
                  pl.BlockSpec((1, D_MODEL), lambda i, f: (0, 0)),
                  pl.BlockSpec((D_MODEL, tf), lambda i, f: (0, f)),
                  pl.BlockSpec((D_MODEL, tf), lambda i, f: (0, f)),
                  pl.BlockSpec((tf, D_MODEL), lambda i, f: (f, 0))],
        out_specs=pl.BlockSpec((tm, D_MODEL), lambda i, f: (i, 0)),
        out_shape=jax.ShapeDtypeStruct((N_TOK, D_MODEL), F32),
        scratch_shapes=[pltpu.VMEM((tm, D_MODEL), BF16), pltpu.VMEM((tm, D_MODEL), F32)],
        compiler_params=_cparams("arbitrary", "arbitrary"),
        name="dense_ffn",
    )(x, mods, mods, mods, g, wg, wu, wd)


def _top2_combine(logits):
    lane = lax.broadcasted_iota(jnp.int32, logits.shape, 1)
    m1 = jnp.max(logits, axis=-1, keepdims=True)
    i1 = jnp.min(jnp.where(logits == m1, lane, LANE), axis=-1, keepdims=True)
    rest = jnp.where(lane == i1, -jnp.inf, logits)
    m2 = jnp.max(rest, axis=-1, keepdims=True)
    i2 = jnp.min(jnp.where(rest == m2, lane, LANE), axis=-1, keepdims=True)
    e2 = jnp.exp(m2 - m1)
    return jnp.where(lane == i1, 1.0 / (1.0 + e2), 0.0) + jnp.where(lane == i2, e2 / (1.0 + e2), 0.0)


def _moe_kernel(x_ref, sh_ref, sc_ref, g2_ref, g_ref, wr_ref, br_ref, wg_ref, wu_ref, wd_ref, o_ref,
                h_ref, comb_ref, acc_ref):
    e = pl.program_id(1)

    @pl.when(e == 0)
    def _():
        h = _modulated_norm(x_ref[...], g_ref[...], sh_ref[...], sc_ref[...])
        h_ref[...] = h.astype(BF16)
        logits = jnp.dot(h.astype(BF16), wr_ref[...].astype(BF16), preferred_element_type=F32) + br_ref[...]
        comb_ref[...] = _top2_combine(logits)
        acc_ref[...] = jnp.zeros_like(acc_ref)

    h = h_ref[...]
    lane = lax.broadcasted_iota(jnp.int32, comb_ref.shape, 1)
    ce = jnp.sum(jnp.where(lane == e, comb_ref[...], 0.0), axis=-1, keepdims=True)
    a = _silu(jnp.dot(h, wg_ref[...], preferred_element_type=F32)) * jnp.dot(h, wu_ref[...], preferred_element_type=F32)
    acc_ref[...] += ce * jnp.dot(a.astype(BF16), wd_ref[...], preferred_element_type=F32)

    @pl.when(e == pl.num_programs(1) - 1)
    def _():
        o_ref[...] = x_ref[...] + g2_ref[...] * acc_ref[...]


def _moe_ffn(x, mods, layer, g, wr, br, wg, wu, wd):
    tm = 512
    ew = lambda: pl.BlockSpec((None, D_MODEL, FF_EXPERT), lambda i, e: (e, 0, 0))
    return pl.pallas_call(
        _moe_kernel,
        grid=(N_TOK // tm, N_EXPERTS),
        in_specs=[pl.BlockSpec((tm, D_MODEL), lambda i, e: (i, 0)),
                  _mod_spec(layer, 3, tm), _mod_spec(layer, 4, tm), _mod_spec(layer, 5, tm),
                  pl.BlockSpec((1, D_MODEL), lambda i, e: (0, 0)),
                  pl.BlockSpec((D_MODEL, LANE), lambda i, e: (0, 0)),
                  pl.BlockSpec((1, LANE), lambda i, e: (0, 0)),
                  ew(), ew(), ew()],
        out_specs=pl.BlockSpec((tm, D_MODEL), lambda i, e: (i, 0)),
        out_shape=jax.ShapeDtypeStruct((N_TOK, D_MODEL), F32),
        scratch_shapes=[pltpu.VMEM((tm, D_MODEL), BF16), pltpu.VMEM((tm, LANE), F32), pltpu.VMEM((tm, D_MODEL), F32)],
        compiler_params=_cparams("arbitrary", "arbitrary"),
        name="moe_ffn",
    )(x, mods, mods, mods, g, wr, br, wg, wu, wd)


def _reorder_w_in(w):
    pad = jnp.zeros((w.shape[0], PROJ_W - (COL_AB + N_GB)), w.dtype)
    return jnp.concatenate([w[:, 0:1536], w[:, 2304:3840], w[:, 1536:2048], w[:, 3840:4352], w[:, 4368:7440],
                            w[:, 2048:2304], w[:, 4352:4368], pad], axis=1).astype(BF16)


def _pad_lanes(v, fill=0.0):
    v = v.reshape(1, -1)
    return jnp.concatenate([v, jnp.full((1, LANE - v.shape[1]), fill, v.dtype)], axis=1)


def kernel(x_prompt, x_sample, cache_k, cache_v, state_fwd, state_bwd, c, c_ctx, norm1_g, norm2_g, w_ada, b_ada, w_in, conv_a_w, q_norm_g, k_norm_g, attn_sink, dn_conv_w, dn_a_log, dn_dt_bias, dn_norm_g, w_out_a, w_out_b, w_out_c, w_o, w_ff_gate, w_ff_up, w_ff_down, w_router, b_router, w_moe_gate, w_moe_up, w_moe_down):
    x = jnp.concatenate([x_prompt.reshape(P_TOK, D_MODEL), x_sample.reshape(S_TOK, D_MODEL)], axis=0)
    cmat = jnp.concatenate([c_ctx[None, :], c, jnp.zeros((SUBLANE - 1 - DEC_BATCH, D_MODEL), F32)], axis=0)
    mods = _modulations(cmat, w_ada, b_ada)
    cos_t, sin_t = _rope_tables()
    ck = cache_k.reshape(DEC_BATCH, DEPTH, PAST_LEN, LANE)
    cv = cache_v.reshape(DEC_BATCH, DEPTH, PAST_LEN, LANE)
    zero_state = jnp.zeros((BATCH, DN_HEADS, DN_DK, DN_DV), F32)

    ks_out, vs_out, sf_out, sb_out = [], [], [], []
    for l in range(DEPTH):
        proj = _input_projection(x, mods, l, norm1_g[l][None, :], _reorder_w_in(w_in[l]))

        qs, kx, vx, k_norm, v_raw = _qk_prep(proj, cos_t, sin_t, jnp.tile(q_norm_g[l], 2)[None, :],
                                             jnp.tile(k_norm_g[l], 2)[None, :])
        y_b = jnp.concatenate([_context_attention(attn_sink[l], qs, kx, vx),
                               _latent_attention(attn_sink[l], qs, kx, vx, ck, cv, l)], axis=0)
        ks_out.append(k_norm[:P_TOK].reshape(BATCH, SEQ, KV_HEADS, HEAD_DIM))
        vs_out.append(v_raw[:P_TOK].reshape(BATCH, SEQ, KV_HEADS, HEAD_DIM))

        local = _dn_local(proj, dn_conv_w[l], _pad_lanes(dn_a_log[l]), _pad_lanes(dn_dt_bias[l]))
        s0f = jnp.concatenate([zero_state, state_fwd[:, l]], axis=0)
        s0b = jnp.concatenate([zero_state, state_bwd[:, l]], axis=0)
        o_f, o_b, s_f, s_b = _dn_scan(*local, s0f, s0b)
        sf_out.append(s_f[:BATCH])
        sb_out.append(s_b[:BATCH])

        x = _merge(proj, y_b, o_f, o_b, x, mods, l, conv_a_w[l], dn_norm_g[l][None, :],
                   w_out_a[l].astype(BF16), w_out_b[l].astype(BF16), w_out_c[l].astype(BF16), w_o[l].astype(BF16))

        i = l // 2
        if l % 2 == 0:
            x = _dense_ffn(x, mods, l, norm2_g[l][None, :], w_ff_gate[i].astype(BF16), w_ff_up[i].astype(BF16),
                           w_ff_down[i].astype(BF16))
        else:
            wr = jnp.concatenate([w_router[i], jnp.zeros((D_MODEL, LANE - N_EXPERTS), F32)], axis=1)
            x = _moe_ffn(x, mods, l, norm2_g[l][None, :], wr, _pad_lanes(b_router[i], -jnp.inf),
                         w_moe_gate[i].astype(BF16), w_moe_up[i].astype(BF16), w_moe_down[i].astype(BF16))

    return (x[:P_TOK].reshape(BATCH, SEQ, D_MODEL), x[P_TOK:].reshape(DEC_BATCH, DEC_SEQ, D_MODEL),
            jnp.stack(ks_out, axis=1), jnp.stack(vs_out, axis=1), jnp.stack(sf_out, axis=1), jnp.stack(sb_out, axis=1))
```

```python
import functools

import jax
import jax.numpy as jnp
from jax import lax
from jax.experimental import pallas as pl
from jax.experimental.pallas import tpu as pltpu

F32 = jnp.float32
BF16 = jnp.bfloat16

D_MODEL = 1024
BATCH = 32
SEQ = 256
DEPTH = 4
DEC_BATCH = 4
DEC_SEQ = 4096
PAST_LEN = 256
GRID_W = 64
A_WIDTH = 512
HEAD_DIM = 64
H_ATT = 8
KV_HEADS = 2
ROPE_BASE = 10000.0
DN_HEADS = 4
DN_DK = 128
DN_DV = 128
DN_CHUNK = 64
FF_DENSE = 2816
N_EXPERTS = 8
FF_EXPERT = 1024
N_MOD = 6
EPS = 1e-6
NEG = -1e30

P_TOK = BATCH * SEQ
S_TOK = DEC_BATCH * DEC_SEQ
N_TOK = P_TOK + S_TOK
N_SEQ = BATCH + DEC_BATCH
LANE = 128
SUBLANE = 8

COL_A = 0
COL_DQKV = 1536
COL_Q = 3072
COL_DZ = 3584
COL_GATES = 4096
COL_KV = 7168
COL_AB = 7424
PROJ_W = 7680

VMEM_LIMIT = 56 * 1024 * 1024


def _cparams(*sem):
    return pltpu.CompilerParams(dimension_semantics=sem, vmem_limit_bytes=VMEM_LIMIT)


def _group_of_tile(i, tm):
    p_tiles = P_TOK // tm
    return jnp.where(i < p_tiles, 0, 1 + (i - p_tiles) // (DEC_SEQ // tm))


def _mod_spec(layer, k, tm):
    return pl.BlockSpec((None, None, None, 1, D_MODEL), lambda i, *_: (layer, _group_of_tile(i, tm), k, 0, 0))


def _seq_pos(i, tm):
    row = i * tm + lax.broadcasted_iota(jnp.int32, (tm, 1), 0)
    seqlen = jnp.where(i * tm < P_TOK, SEQ, DEC_SEQ)
    return row & (seqlen - 1), seqlen


HALO = 16


def _halo_specs(tm, width, col_block):
    r = tm // HALO
    last = N_TOK // HALO - 1
    prev = pl.BlockSpec((HALO, width), lambda i, *_: (jnp.maximum(i * r - 1, 0), col_block))
    nxt = pl.BlockSpec((HALO, width), lambda i, *_: (jnp.minimum((i + 1) * r, last), col_block))
    return prev, nxt


def _dwconv3(x, prev_row, next_row, w, pos, seqlen):
    tm = x.shape[0]
    r = lax.broadcasted_iota(jnp.int32, (tm, 1), 0)
    xp = jnp.where(r == 0, prev_row, pltpu.roll(x, 1, 0))
    xn = jnp.where(r == tm - 1, next_row, pltpu.roll(x, tm - 1, 0))
    xp = jnp.where(pos == 0, 0.0, xp)
    xn = jnp.where(pos == seqlen - 1, 0.0, xn)
    return xp * w[0:1, :] + x * w[1:2, :] + xn * w[2:3, :]


def _silu(x):
    return x * jax.nn.sigmoid(x)


def _modulated_norm(x, g, shift, scale):
    y = x * lax.rsqrt(jnp.mean(x * x, axis=-1, keepdims=True) + EPS)
    return (y * g) * (1.0 + scale) + shift


def _mod_kernel(c_ref, w_ref, b_ref, o_ref):
    s = _silu(c_ref[...])
    o_ref[...] = jnp.dot(s.astype(BF16), w_ref[...].astype(BF16), preferred_element_type=F32) + b_ref[...]


def _modulations(cmat, w_ada, b_ada):
    tn = 1024
    out = pl.pallas_call(
        _mod_kernel,
        grid=(DEPTH, N_MOD * D_MODEL // tn),
        in_specs=[pl.BlockSpec((SUBLANE, D_MODEL), lambda l, j: (0, 0)),
                  pl.BlockSpec((None, D_MODEL, tn), lambda l, j: (l, 0, j)),
                  pl.BlockSpec((None, 1, tn), lambda l, j: (l, 0, j))],
        out_specs=pl.BlockSpec((None, SUBLANE, tn), lambda l, j: (l, 0, j)),
        out_shape=jax.ShapeDtypeStruct((DEPTH, SUBLANE, N_MOD * D_MODEL), F32),
        compiler_params=_cparams("arbitrary", "arbitrary"),
        name="ada_modulation",
    )(cmat, w_ada, b_ada.reshape(DEPTH, 1, N_MOD * D_MODEL))
    return out.reshape(DEPTH, SUBLANE, N_MOD, 1, D_MODEL)


def _proj_kernel(x_ref, sh_ref, sc_ref, g_ref, w_ref, o_ref, h_ref):
    @pl.when(pl.program_id(1) == 0)
    def _():
        h_ref[...] = _modulated_norm(x_ref[...], g_ref[...], sh_ref[...], sc_ref[...]).astype(BF16)

    o_ref[...] = jnp.dot(h_ref[...], w_ref[...], preferred_element_type=F32).astype(o_ref.dtype)


def _input_projection(x, mods, layer, g, w):
    tm, tn = 1024, 3840
    return pl.pallas_call(
        _proj_kernel,
        grid=(N_TOK // tm, PROJ_W // tn),
        in_specs=[pl.BlockSpec((tm, D_MODEL), lambda i, j: (i, 0)),
                  _mod_spec(layer, 0, tm), _mod_spec(layer, 1, tm),
                  pl.BlockSpec((1, D_MODEL), lambda i, j: (0, 0)),
                  pl.BlockSpec((D_MODEL, tn), lambda i, j: (0, j))],
        out_specs=pl.BlockSpec((tm, tn), lambda i, j: (i, j)),
        out_shape=jax.ShapeDtypeStruct((N_TOK, PROJ_W), BF16),
        scratch_shapes=[pltpu.VMEM((tm, D_MODEL), BF16)],
        compiler_params=_cparams("arbitrary", "arbitrary"),
        name="input_projection",
    )(x, mods, mods, g, w)


def _pair_norm(x, gain):
    lane = lax.broadcasted_iota(jnp.int32, x.shape, 1)
    x2 = x * x
    lo = jnp.sum(jnp.where(lane < HEAD_DIM, x2, 0.0), axis=-1, keepdims=True)
    hi = jnp.sum(jnp.where(lane >= HEAD_DIM, x2, 0.0), axis=-1, keepdims=True)
    inv = jnp.where(lane < HEAD_DIM, lax.rsqrt(lo / HEAD_DIM + EPS), lax.rsqrt(hi / HEAD_DIM + EPS))
    return x * inv * gain


def _pair_rope(x, cos, sin):
    lane = lax.broadcasted_iota(jnp.int32, x.shape, 1)
    swapped = jnp.where((lane & 1) == 0, pltpu.roll(x, LANE - 1, 1), pltpu.roll(x, 1, 1))
    return x * cos + swapped * sin


def _lane_halves(x):
    lane = lax.broadcasted_iota(jnp.int32, x.shape, 1)
    xr = pltpu.roll(x, HEAD_DIM, 1)
    lo, hi = lane < HEAD_DIM, lane >= HEAD_DIM
    return (jnp.where(lo, x, 0.0), jnp.where(hi, xr, 0.0), jnp.where(lo, xr, 0.0), jnp.where(hi, x, 0.0))


def _qk_prep_kernel(q_ref, kv_ref, cos_ref, sin_ref, qg_ref, kg_ref, qo_ref, ko_ref, vo_ref, kn_ref, v_ref):
    cos, sin = cos_ref[...], sin_ref[...]
    for p in range(H_ATT // 2):
        qn = _pair_norm(q_ref[:, p * LANE:(p + 1) * LANE].astype(F32), qg_ref[...])
        qo_ref[:, p * LANE:(p + 1) * LANE] = (_pair_rope(qn, cos, sin) * (HEAD_DIM ** -0.5)).astype(BF16)
    k = _pair_norm(kv_ref[:, 0:LANE].astype(F32), kg_ref[...])
    v = kv_ref[:, LANE:2 * LANE].astype(F32)
    kn_ref[...] = k
    v_ref[...] = v
    for j, (kx, vx) in enumerate(zip(_lane_halves(_pair_rope(k, cos, sin)), _lane_halves(v))):
        ko_ref[:, j * LANE:(j + 1) * LANE] = kx.astype(BF16)
        vo_ref[:, j * LANE:(j + 1) * LANE] = vx.astype(BF16)


def _qk_prep(proj, cos_t, sin_t, qg, kg):
    tm = 256
    tab = lambda i: jnp.where(i < P_TOK // tm, 0, 1 + (i - P_TOK // tm) % (DEC_SEQ // tm))
    wide = jax.ShapeDtypeStruct((N_TOK, 4 * LANE), BF16)
    slab = jax.ShapeDtypeStruct((N_TOK, LANE), F32)
    return pl.pallas_call(
        _qk_prep_kernel,
        grid=(N_TOK // tm,),
        in_specs=[pl.BlockSpec((tm, 4 * LANE), lambda i: (i, COL_Q // (4 * LANE))),
                  pl.BlockSpec((tm, 2 * LANE), lambda i: (i, COL_KV // (2 * LANE))),
                  pl.BlockSpec((tm, LANE), lambda i: (tab(i), 0)),
                  pl.BlockSpec((tm, LANE), lambda i: (tab(i), 0)),
                  pl.BlockSpec((1, LANE), lambda i: (0, 0)),
                  pl.BlockSpec((1, LANE), lambda i: (0, 0))],
        out_specs=[pl.BlockSpec((tm, 4 * LANE), lambda i: (i, 0)),
                   pl.BlockSpec((tm, 4 * LANE), lambda i: (i, 0)),
                   pl.BlockSpec((tm, 4 * LANE), lambda i: (i, 0)),
                   pl.BlockSpec((tm, LANE), lambda i: (i, 0)),
                   pl.BlockSpec((tm, LANE), lambda i: (i, 0))],
        out_shape=[wide, wide, wide, slab, slab],
        compiler_params=_cparams("arbitrary"),
        name="qk_prep",
    )(proj, proj, cos_t, sin_t, qg, kg)


def _rope_tables():
    pos = jnp.arange(DEC_SEQ)
    row = (pos // GRID_W).astype(F32)
    col = (pos % GRID_W).astype(F32)
    n_freq = HEAD_DIM // 4
    inv_freq = ROPE_BASE ** (-jnp.arange(n_freq, dtype=F32) / n_freq)
    ang = jnp.concatenate([row[:, None] * inv_freq, col[:, None] * inv_freq], axis=-1)
    cos = jnp.repeat(jnp.cos(ang), 2, axis=-1)
    sin = jnp.stack([-jnp.sin(ang), jnp.sin(ang)], axis=-1).reshape(DEC_SEQ, HEAD_DIM)
    cos = jnp.concatenate([jnp.ones((SEQ, HEAD_DIM), F32), cos], axis=0)
    sin = jnp.concatenate([jnp.zeros((SEQ, HEAD_DIM), F32), sin], axis=0)
    return jnp.tile(cos, (1, 2)), jnp.tile(sin, (1, 2))


def _attend_kv_head(q2, kx, vx, sink_a, sink_b, mask, kvh):
    nq = q2.shape[0] // 2
    row = lax.broadcasted_iota(jnp.int32, (2 * nq, 1), 0)
    out = None
    for half in range(2):
        j = 2 * kvh + half
        s = lax.dot_general(q2, kx[:, j * LANE:(j + 1) * LANE], (((1,), (1,)), ((), ())), preferred_element_type=F32)
        if mask is not None:
            s = jnp.where(mask, s, NEG)
        sink = jnp.where(row < nq, sink_a[half], sink_b[half])
        m = jnp.maximum(jnp.max(s, axis=-1, keepdims=True), sink)
        e = jnp.exp(s - m)
        denom = jnp.sum(e, axis=-1, keepdims=True) + jnp.exp(sink - m)
        o = jnp.dot(e.astype(BF16), vx[:, j * LANE:(j + 1) * LANE], preferred_element_type=F32) / denom
        out = o if out is None else out + o
    return out


def _ctx_attn_kernel(sink_ref, q_ref, k_ref, v_ref, o_ref):
    kx, vx = k_ref[...], v_ref[...]
    for kvh in range(KV_HEADS):
        p0, p1 = 2 * kvh, 2 * kvh + 1
        q2 = jnp.concatenate([q_ref[:, p0 * LANE:(p0 + 1) * LANE], q_ref[:, p1 * LANE:(p1 + 1) * LANE]], axis=0)
        sa = (sink_ref[2 * p0], sink_ref[2 * p0 + 1])
        sb = (sink_ref[2 * p1], sink_ref[2 * p1 + 1])
        o = _attend_kv_head(q2, kx, vx, sa, sb, None, kvh)
        o_ref[:, p0 * LANE:(p0 + 1) * LANE] = o[:SEQ]
        o_ref[:, p1 * LANE:(p1 + 1) * LANE] = o[SEQ:]


def _context_attention(sink, qs, kx, vx):
    blk = lambda: pl.BlockSpec((SEQ, 4 * LANE), lambda b: (b, 0))
    return pl.pallas_call(
        _ctx_attn_kernel,
        grid=(BATCH,),
        in_specs=[pl.BlockSpec(memory_space=pltpu.SMEM), blk(), blk(), blk()],
        out_specs=blk(),
        out_shape=jax.ShapeDtypeStruct((P_TOK, 4 * LANE), F32),
        compiler_params=_cparams("arbitrary"),
        name="context_attention",
    )(sink, qs, kx, vx)


LAT_BLOCK = 128


def _lat_attn_kernel(sink_ref, q_ref, kp_ref, kc_ref, kn_ref, vp_ref, vc_ref, vn_ref, ck_ref, cv_ref, o_ref):
    n = pl.program_id(1)
    nb = pl.num_programs(1)
    ck = jnp.concatenate(_lane_halves(ck_ref[...]), axis=1).astype(BF16)
    cv = jnp.concatenate(_lane_halves(cv_ref[...]), axis=1).astype(BF16)
    kx = jnp.concatenate([kp_ref[...], kc_ref[...], kn_ref[...], ck], axis=0)
    vx = jnp.concatenate([vp_ref[...], vc_ref[...], vn_ref[...], cv], axis=0)
    n_keys = 3 * LAT_BLOCK + PAST_LEN
    qi = lax.broadcasted_iota(jnp.int32, (2 * LAT_BLOCK, n_keys), 0) & (LAT_BLOCK - 1)
    kj = lax.broadcasted_iota(jnp.int32, (2 * LAT_BLOCK, n_keys), 1)
    far = 2 * LAT_BLOCK
    off_prev = jnp.where(n > 0, 0, far)
    off_next = jnp.where(n < nb - 1, 0, far)
    t_prev = jnp.where(kj < LAT_BLOCK, kj - qi - off_prev, 0)
    t_next = jnp.where(kj >= 2 * LAT_BLOCK, jnp.where(kj < 3 * LAT_BLOCK, qi - off_next - (kj - 2 * LAT_BLOCK), 0), 0)
    mask = jnp.minimum(t_prev, t_next) >= 0
    for kvh in range(KV_HEADS):
        p0, p1 = 2 * kvh, 2 * kvh + 1
        q2 = jnp.concatenate([q_ref[:, p0 * LANE:(p0 + 1) * LANE], q_ref[:, p1 * LANE:(p1 + 1) * LANE]], axis=0)
        sa = (sink_ref[2 * p0], sink_ref[2 * p0 + 1])
        sb = (sink_ref[2 * p1], sink_ref[2 * p1 + 1])
        o = _attend_kv_head(q2, kx, vx, sa, sb, mask, kvh)
        o_ref[:, p0 * LANE:(p0 + 1) * LANE] = o[:LAT_BLOCK]
        o_ref[:, p1 * LANE:(p1 + 1) * LANE] = o[LAT_BLOCK:]


def _latent_attention(sink, qs, kx, vx, cache_k, cache_v, layer):
    nb = DEC_SEQ // LAT_BLOCK
    base = P_TOK // LAT_BLOCK
    cur = lambda b, n: (base + b * nb + n, 0)
    prev = lambda b, n: (base + b * nb + jnp.maximum(n - 1, 0), 0)
    nxt = lambda b, n: (base + b * nb + jnp.minimum(n + 1, nb - 1), 0)
    blk = lambda f: pl.BlockSpec((LAT_BLOCK, 4 * LANE), f)
    ctx = pl.BlockSpec((None, None, PAST_LEN, LANE), lambda b, n: (b, layer, 0, 0))
    return pl.pallas_call(
        _lat_attn_kernel,
        grid=(DEC_BATCH, nb),
        in_specs=[pl.BlockSpec(memory_space=pltpu.SMEM), blk(cur),
                  blk(prev), blk(cur), blk(nxt), blk(prev), blk(cur), blk(nxt), ctx, ctx],
        out_specs=pl.BlockSpec((LAT_BLOCK, 4 * LANE), lambda b, n: (b * nb + n, 0)),
        out_shape=jax.ShapeDtypeStruct((S_TOK, 4 * LANE), F32),
        compiler_params=_cparams("arbitrary", "arbitrary"),
        name="latent_attention",
    )(sink, qs, kx, kx, kx, vx, vx, vx, cache_k, cache_v)


DN_W = DN_HEADS * DN_DK
N_GB = 2 * 2 * DN_HEADS
DN_ROWS = DN_HEADS * DN_CHUNK
DN_TILE = SEQ
CH_PER_TILE = DN_TILE // DN_CHUNK
SEQ_GROUP = DEC_BATCH
P_GC = (BATCH // SEQ_GROUP) * (SEQ // DN_CHUNK)
N_GC = P_GC + DEC_SEQ // DN_CHUNK


def _dot_hi(a, b):
    return jnp.dot(a, b, precision=lax.Precision.HIGHEST, preferred_element_type=F32)


def _dot_nt(a, b):
    return lax.dot_general(a.astype(BF16), b.astype(BF16), (((1,), (1,)), ((), ())), preferred_element_type=F32)


def _dot_tn(a, b):
    return lax.dot_general(a.astype(BF16), b.astype(BF16), (((0,), (0,)), ((), ())), preferred_element_type=F32)


def _dot_bf(a, b):
    return jnp.dot(a.astype(BF16), b.astype(BF16), preferred_element_type=F32)


def _stack_heads(x):
    return jnp.concatenate([x[:, h * DN_DK:(h + 1) * DN_DK] for h in range(DN_HEADS)], axis=0)


def _stack_cols(m, c0):
    return jnp.concatenate([m[:, c0 + h:c0 + h + 1] for h in range(DN_HEADS)], axis=0)


def _head(x, h):
    return x[h * DN_CHUNK:(h + 1) * DN_CHUNK]


def _per_head(f):
    return jnp.concatenate([f(h) for h in range(DN_HEADS)], axis=0)


def _tri_inverses(lms, xor):
    eye = (xor == 0).astype(F32)
    ts = [eye - jnp.where(xor < 2, lm, 0.0) for lm in lms]
    k = 1
    while (2 << k) <= DN_CHUNK:
        level = (xor >> k) == 1
        ms = [jnp.where(level, lm, 0.0).astype(BF16) for lm in lms]
        tbs = [t.astype(BF16) for t in ts]
        mts = [_per_head(lambda h: _dot_bf(_head(m, h), _head(tb, h))).astype(BF16) for m, tb in zip(ms, tbs)]
        ts = [t - _per_head(lambda h: _dot_bf(_head(tb, h), _head(mt, h))) for t, tb, mt in zip(ts, tbs, mts)]
        k += 1
    return ts


def _dn_tile_slot(i):
    p_tiles = P_TOK // DN_TILE
    tiles_per_seq = DEC_SEQ // DN_TILE
    j = i - p_tiles
    blk = jnp.where(i < p_tiles, i // SEQ_GROUP, P_GC // CH_PER_TILE + j % tiles_per_seq)
    q = jnp.where(i < p_tiles, i % SEQ_GROUP, j // tiles_per_seq)
    return blk, q


def _dn_local_kernel(x_ref, xp_ref, xn_ref, ab_ref, w_ref, alog_ref, dtb_ref,
                     u_ref, w_out_ref, qd_ref, kd_ref, qk_ref, egl_ref):
    pos, seqlen = _seq_pos(pl.program_id(0), DN_TILE)
    s = _silu(_dwconv3(x_ref[...].astype(F32), xp_ref[...].astype(F32)[HALO - 1:HALO, :],
                       xn_ref[...].astype(F32)[0:1, :], w_ref[...], pos, seqlen))
    ab = ab_ref[...].astype(F32)
    lane = lax.broadcasted_iota(jnp.int32, ab.shape, 1)
    z = ab + dtb_ref[...]
    softplus = jnp.maximum(z, 0.0) + jnp.log1p(jnp.exp(-jnp.abs(z)))
    gb = jnp.where(lane < N_GB // 2, -jnp.exp(alog_ref[...]) * softplus, jax.nn.sigmoid(ab))

    c, r = DN_CHUNK, DN_ROWS
    ri = lax.broadcasted_iota(jnp.int32, (r, c), 0) & (c - 1)
    ci = lax.broadcasted_iota(jnp.int32, (r, c), 1)
    xor = ri ^ ci
    masks = ((ri >= ci, ri > ci), (ri <= ci, ri < ci))
    tri = (lax.broadcasted_iota(jnp.int32, (c, c), 0) >= lax.broadcasted_iota(jnp.int32, (c, c), 1)).astype(F32)

    problems, lms = [], []
    for ch in range(CH_PER_TILE):
        sc = s[ch * c:(ch + 1) * c]
        qst = _stack_heads(sc[:, 0:DN_W])
        kst = _stack_heads(sc[:, DN_W:2 * DN_W])
        vst = _stack_heads(sc[:, 2 * DN_W:3 * DN_W])
        qst = qst * lax.rsqrt(jnp.sum(qst * qst, axis=-1, keepdims=True) + EPS) * (DN_DK ** -0.5)
        kst = kst * lax.rsqrt(jnp.sum(kst * kst, axis=-1, keepdims=True) + EPS)
        gbc = gb[ch * c:(ch + 1) * c]
        prefix = _dot_hi(tri, gbc)
        suffix = prefix[c - 1:c] - prefix + gbc
        qb, kb = qst.astype(BF16), kst.astype(BF16)
        gram = _per_head(lambda h: _dot_nt(_head(kb, h), _head(kb, h)))
        qk0 = _per_head(lambda h: _dot_nt(_head(qb, h), _head(kb, h)))
        for d, (gc, last) in enumerate(((prefix, c - 1), (suffix, 0))):
            incl, strict = masks[d]
            c0 = d * DN_HEADS
            gcol = _stack_cols(gc, c0)
            beta = _stack_cols(gbc, N_GB // 2 + c0)
            gct = gc.T
            grow = _per_head(lambda h: jnp.broadcast_to(gct[c0 + h:c0 + h + 1, :], (c, c)))
            decay = jnp.exp(jnp.where(incl, gcol - grow, NEG))
            lms.append(jnp.where(strict, gram * beta * decay, 0.0))
            eg = jnp.exp(gcol)
            glast = _per_head(lambda h: jnp.broadcast_to(gc[last:last + 1, c0 + h:c0 + h + 1], (c, 1)))
            qd_ref[ch, d] = (qst * eg).astype(BF16)
            kd_ref[ch, d] = (kst * jnp.exp(glast - gcol)).astype(BF16)
            qk_ref[ch, d] = jnp.where(incl, qk0 * decay, 0.0).astype(BF16)
            egl_ref[ch, d] = jnp.concatenate([jnp.broadcast_to(jnp.exp(gc[last:last + 1, c0 + h:c0 + h + 1]), (1, DN_DV))
                                              for h in range(DN_HEADS)], axis=1)
            problems.append((ch, d, jnp.concatenate([vst * beta, kst * (beta * eg)], axis=1).astype(BF16)))

    for (ch, d, rhs), t in zip(problems, _tri_inverses(lms, xor)):
        tb = t.astype(BF16)
        x = _per_head(lambda h: jnp.dot(_head(tb, h), _head(rhs, h), preferred_element_type=F32))
        u_ref[ch, d] = x[:, 0:DN_DV]
        w_out_ref[ch, d] = x[:, DN_DV:2 * DN_DV].astype(BF16)


def _dn_local(proj, conv_w, alog, dtb):
    wq = 3 * DN_W
    prev, nxt = _halo_specs(DN_TILE, wq, COL_DQKV // wq)

    def slot_spec(*tail):
        return pl.BlockSpec((CH_PER_TILE, None, 2) + tail,
                            lambda i: _dn_tile_slot(i) + (0,) * (1 + len(tail)))

    def slot_shape(dtype, *tail):
        return jax.ShapeDtypeStruct((N_GC, SEQ_GROUP, 2) + tail, dtype)

    return pl.pallas_call(
        _dn_local_kernel,
        grid=(N_TOK // DN_TILE,),
        in_specs=[pl.BlockSpec((DN_TILE, wq), lambda i: (i, COL_DQKV // wq)), prev, nxt,
                  pl.BlockSpec((DN_TILE, LANE), lambda i: (i, COL_AB // LANE)),
                  pl.BlockSpec((3, wq), lambda i: (0, 0)),
                  pl.BlockSpec((1, LANE), lambda i: (0, 0)),
                  pl.BlockSpec((1, LANE), lambda i: (0, 0))],
        out_specs=[slot_spec(DN_ROWS, DN_DV), slot_spec(DN_ROWS, DN_DK), slot_spec(DN_ROWS, DN_DK),
                   slot_spec(DN_ROWS, DN_DK), slot_spec(DN_ROWS, DN_CHUNK), slot_spec(1, DN_W)],
        out_shape=[slot_shape(F32, DN_ROWS, DN_DV), slot_shape(BF16, DN_ROWS, DN_DK), slot_shape(BF16, DN_ROWS, DN_DK),
                   slot_shape(BF16, DN_ROWS, DN_DK), slot_shape(BF16, DN_ROWS, DN_CHUNK), slot_shape(F32, 1, DN_W)],
        compiler_params=_cparams("arbitrary"),
        name="deltanet_local",
    )(proj, proj, proj, proj, conv_w, alog, dtb)


def _scan_step(t):
    pc, sc = SEQ // DN_CHUNK, DEC_SEQ // DN_CHUNK
    is_p = t < P_GC
    g = jnp.where(is_p, t // pc, BATCH // SEQ_GROUP)
    s = jnp.where(is_p, t % pc, t - P_GC)
    n = jnp.where(is_p, pc, sc)
    return g, s, n, t, t - s + (n - 1 - s)


def _dn_scan_kernel(uf_ref, wf_ref, qdf_ref, kdf_ref, qkf_ref, egf_ref, ub_ref, wb_ref, qdb_ref, kdb_ref, qkb_ref,
                    egb_ref, s0f_ref, s0b_ref, of_ref, ob_ref, sf_ref, sb_ref, state_ref):
    _, s, n, _, _ = _scan_step(pl.program_id(0))
    c, r = DN_CHUNK, DN_ROWS
    heads = range(DN_HEADS)

    @pl.when(s == 0)
    def _():
        for q in range(SEQ_GROUP):
            state_ref[0, q] = jnp.concatenate([s0f_ref[q, h] for h in heads], axis=1)
            state_ref[1, q] = jnp.concatenate([s0b_ref[q, h] for h in heads], axis=1)

    head_slab = (lax.broadcasted_iota(jnp.int32, (r, DN_W), 0) // c) == (lax.broadcasted_iota(jnp.int32, (r, DN_W), 1) // DN_DV)
    dirs = ((uf_ref, wf_ref, qdf_ref, kdf_ref, qkf_ref, egf_ref, of_ref),
            (ub_ref, wb_ref, qdb_ref, kdb_ref, qkb_ref, egb_ref, ob_ref))
    chains = [(d, q) + refs for d, refs in enumerate(dirs) for q in range(SEQ_GROUP)]
    states = [state_ref[d, q] for d, q, *_ in chains]
    boths = [jnp.dot(jnp.concatenate([w_ref[q], qd_ref[q]], axis=0), st.astype(BF16), preferred_element_type=F32)
             for (d, q, u_ref, w_ref, qd_ref, *_), st in zip(chains, states)]
    v_news = [(u_ref[q] - jnp.concatenate([b[h * c:(h + 1) * c, h * DN_DV:(h + 1) * DN_DV] for h in heads], axis=0)
               ).astype(BF16) for (d, q, u_ref, *_), b in zip(chains, boths)]
    for (d, q, u_ref, w_ref, qd_ref, kd_ref, qk_ref, eg_ref, o_ref), st, b, v_new in zip(chains, states, boths, v_news):
        qs = jnp.concatenate([b[r + h * c:r + (h + 1) * c, h * DN_DV:(h + 1) * DN_DV] for h in heads], axis=0)
        qk = qk_ref[q]
        o = qs + _per_head(lambda h: jnp.dot(_head(qk, h), _head(v_new, h), preferred_element_type=F32))
        v_wide = jnp.concatenate([v_new] * DN_HEADS, axis=1)
        v_wide = jnp.where(head_slab, v_wide, jnp.zeros_like(v_wide))
        state_ref[d, q] = st * eg_ref[q] + lax.dot_general(kd_ref[q], v_wide, (((0,), (0,)), ((), ())),
                                                           preferred_element_type=F32)
        o_ref[q] = jnp.concatenate([o[h * c:(h + 1) * c] for h in heads], axis=1)

    @pl.when(s == n - 1)
    def _():
        for q in range(SEQ_GROUP):
            for h in heads:
                sf_ref[q, h] = state_ref[0, q][:, h * DN_DV:(h + 1) * DN_DV]
                sb_ref[q, h] = state_ref[1, q][:, h * DN_DV:(h + 1) * DN_DV]


def _dn_scan(u, w, qd, kd, qk, egl, s0f, s0b):
    fwd = lambda t: _scan_step(t)[3]
    bwd = lambda t: _scan_step(t)[4]
    grp = lambda t: _scan_step(t)[0]

    def specs(slot, d):
        sp = lambda *tail: pl.BlockSpec((None, SEQ_GROUP, None) + tail, lambda t: (slot(t), 0, d) + (0,) * len(tail))
        return [sp(DN_ROWS, DN_DV), sp(DN_ROWS, DN_DK), sp(DN_ROWS, DN_DK), sp(DN_ROWS, DN_DK), sp(DN_ROWS, DN_CHUNK),
                sp(1, DN_W)]

    st = lambda: pl.BlockSpec((SEQ_GROUP, DN_HEADS, DN_DK, DN_DV), lambda t: (grp(t), 0, 0, 0))
    o_spec = lambda slot: pl.BlockSpec((None, SEQ_GROUP, DN_CHUNK, DN_W), lambda t: (slot(t), 0, 0, 0))
    o_shape = jax.ShapeDtypeStruct((N_GC, SEQ_GROUP, DN_CHUNK, DN_W), F32)
    st_shape = jax.ShapeDtypeStruct((N_SEQ, DN_HEADS, DN_DK, DN_DV), F32)
    args = (u, w, qd, kd, qk, egl)
    return pl.pallas_call(
        _dn_scan_kernel,
        grid=(N_GC,),
        in_specs=specs(fwd, 0) + specs(bwd, 1) + [st(), st()],
        out_specs=[o_spec(fwd), o_spec(bwd), st(), st()],
        out_shape=[o_shape, o_shape, st_shape, st_shape],
        scratch_shapes=[pltpu.VMEM((2, SEQ_GROUP, DN_DK, DN_W), F32)],
        compiler_params=_cparams("arbitrary"),
        name="deltanet_scan",
    )(*args, *args, s0f, s0b)


def _merge_kernel(a_ref, ap_ref, an_ref, yb_ref, of_ref, ob_ref, z_ref, ga_ref, gb_ref, gc_ref, x_ref, g1_ref,
                  cw_ref, ng_ref, wa_ref, wb_ref, wc_ref, wo_ref, o_ref):
    tm = x_ref.shape[0]
    pos, seqlen = _seq_pos(pl.program_id(0), tm)
    w = A_WIDTH
    a = a_ref[...].astype(F32)
    ap = ap_ref[...].astype(F32)[HALO - 1:HALO, :]
    an = an_ref[...].astype(F32)[0:1, :]
    u = a[:, w:2 * w] * a[:, 2 * w:3 * w]
    up = ap[:, w:2 * w] * ap[:, 2 * w:3 * w]
    un = an[:, w:2 * w] * an[:, 2 * w:3 * w]
    y_a = a[:, 0:w] * _dwconv3(u, up, un, cw_ref[...], pos, seqlen)
    merged = jax.nn.sigmoid(ga_ref[...].astype(F32)) * _dot_bf(y_a, wa_ref[...])
    merged += jax.nn.sigmoid(gb_ref[...].astype(F32)) * _dot_bf(yb_ref[...], wb_ref[...])
    o_sum = (of_ref[...] + ob_ref[...]).reshape(tm, DN_W)
    parts = []
    for h in range(DN_HEADS):
        sl = slice(h * DN_DV, (h + 1) * DN_DV)
        o = o_sum[:, sl]
        o = o * lax.rsqrt(jnp.mean(o * o, axis=-1, keepdims=True) + EPS) * ng_ref[...]
        parts.append(o * _silu(z_ref[:, sl].astype(F32)))
    y_c = jnp.concatenate(parts, axis=1)
    merged += jax.nn.sigmoid(gc_ref[...].astype(F32)) * _dot_bf(y_c, wc_ref[...])
    o_ref[...] = x_ref[...] + g1_ref[...] * _dot_bf(merged, wo_ref[...])


def _merge(proj, y_b, o_f, o_b, x, mods, layer, conv_a_w, dn_norm_g, wa, wb, wc, wo):
    tm = DN_TILE
    wa3 = 3 * A_WIDTH
    prev, nxt = _halo_specs(tm, wa3, COL_A // wa3)
    tok = lambda w, cb=0: pl.BlockSpec((tm, w), lambda i: (i, cb))
    full = lambda a: pl.BlockSpec(a.shape, lambda i: (0,) * a.ndim)
    scan_out = lambda: pl.BlockSpec((CH_PER_TILE, None, DN_CHUNK, DN_W), lambda i: _dn_tile_slot(i) + (0, 0))
    gate0 = COL_GATES // D_MODEL
    return pl.pallas_call(
        _merge_kernel,
        grid=(N_TOK // tm,),
        in_specs=[tok(wa3, COL_A // wa3), prev, nxt, tok(A_WIDTH), scan_out(), scan_out(), tok(DN_W, COL_DZ // DN_W),
                  tok(D_MODEL, gate0), tok(D_MODEL, gate0 + 1), tok(D_MODEL, gate0 + 2), tok(D_MODEL),
                  _mod_spec(layer, 2, tm), full(conv_a_w), full(dn_norm_g), full(wa), full(wb), full(wc), full(wo)],
        out_specs=tok(D_MODEL),
        out_shape=jax.ShapeDtypeStruct((N_TOK, D_MODEL), F32),
        compiler_params=_cparams("arbitrary"),
        name="mixer_merge",
    )(proj, proj, proj, y_b, o_f, o_b, proj, proj, proj, proj, x, mods, conv_a_w, dn_norm_g, wa, wb, wc, wo)


def _ffn_kernel(x_ref, sh_ref, sc_ref, g2_ref, g_ref, wg_ref, wu_ref, wd_ref, o_ref, h_ref, acc_ref):
    f = pl.program_id(1)

    @pl.when(f == 0)
    def _():
        h_ref[...] = _modulated_norm(x_ref[...], g_ref[...], sh_ref[...], sc_ref[...]).astype(BF16)
        acc_ref[...] = jnp.zeros_like(acc_ref)

    h = h_ref[...]
    a = _silu(jnp.dot(h, wg_ref[...], preferred_element_type=F32)) * jnp.dot(h, wu_ref[...], preferred_element_type=F32)
    acc_ref[...] += jnp.dot(a.astype(BF16), wd_ref[...], preferred_element_type=F32)

    @pl.when(f == pl.num_programs(1) - 1)
    def _():
        o_ref[...] = x_ref[...] + g2_ref[...] * acc_ref[...]


def _dense_ffn(x, mods, layer, g, wg, wu, wd):
    tm, tf = 512, 1408
    return pl.pallas_call(
        _ffn_kernel,
        grid=(N_TOK // tm, FF_DENSE // tf),
        in_specs=[pl.BlockSpec((tm, D_MODEL), lambda i, f: (i, 0)),
                  _mod_spec(layer, 3, tm), _mod_spec(layer, 4, tm), _mod_spec(layer, 5, tm),
                  pl.BlockSpec((1, D_MODEL), lambda i, f: (0, 0)),
                  pl.BlockSpec((D_MODEL, tf), lambda i, f: (0, f)),
                  pl.BlockSpec((D_MODEL, tf), lambda i, f: (0, f)),
                  pl.BlockSpec((tf, D_MODEL), lambda i, f: (f, 0))],
        out_specs=pl.BlockSpec((tm, D_MODEL), lambda i, f: (i, 0)),
        out_shape=jax.ShapeDtypeStruct((N_TOK, D_MODEL), F32),
        scratch_shapes=[pltpu.VMEM((tm, D_MODEL), BF16), pltpu.VMEM((tm, D_MODEL), F32)],
        compiler_params=_cparams("arbitrary", "arbitrary"),
        name="dense_ffn",
    )(x, mods, mods, mods, g, wg, wu, wd)


MOE_TM = 1024
MOE_RB = 256
ROUTE_ROWS = 16


def _moe_kernel(x_ref, sh_ref, sc_ref, g2_ref, g_ref, wr_ref, br_ref, before_ref, wg_ref, wu_ref, wd_ref, o_ref,
                h_ref, route_ref, acc_ref):
    e = pl.program_id(1)
    sub = lax.broadcasted_iota(jnp.int32, (ROUTE_ROWS, MOE_TM), 0)

    @pl.when(e == 0)
    def _():
        hb = _modulated_norm(x_ref[...], g_ref[...], sh_ref[...], sc_ref[...]).astype(BF16)
        h_ref[...] = hb
        logits = lax.dot_general(wr_ref[...].astype(BF16), hb, (((1,), (1,)), ((), ())),
                                 preferred_element_type=F32) + br_ref[:, 0:1]
        m1 = jnp.max(logits, axis=0, keepdims=True)
        i1 = jnp.min(jnp.where(logits == m1, sub, ROUTE_ROWS), axis=0, keepdims=True)
        rest = jnp.where(sub == i1, -jnp.inf, logits)
        m2 = jnp.max(rest, axis=0, keepdims=True)
        i2 = jnp.min(jnp.where(rest == m2, sub, ROUTE_ROWS), axis=0, keepdims=True)
        e2 = jnp.exp(m2 - m1)
        route_ref[0] = jnp.where(sub == i1, 1.0 / (1.0 + e2), 0.0) + jnp.where(sub == i2, e2 / (1.0 + e2), 0.0)
        sel = jnp.where(sub == i1, 1.0, jnp.where(sub == i2, 1.0, 0.0))
        rank = jnp.dot(sel.astype(BF16), before_ref[...], preferred_element_type=F32)
        route_ref[1] = jnp.where(sel > 0.5, rank, -1.0)
        acc_ref[...] = jnp.zeros_like(acc_ref)

    pick = sub == e
    comb_e = jnp.sum(jnp.where(pick, route_ref[0], 0.0), axis=0, keepdims=True)
    key_e = jnp.sum(jnp.where(pick, route_ref[1], 0.0), axis=0, keepdims=True)
    count = jnp.sum(jnp.where(key_e >= 0.0, 1.0, 0.0))

    for j in range(MOE_TM // MOE_RB):
        @pl.when(count > j * MOE_RB)
        def _():
            r = (lax.broadcasted_iota(jnp.int32, (MOE_RB, MOE_TM), 0) + j * MOE_RB).astype(F32)
            hit = key_e == r
            onehot = jnp.where(hit, 1.0, 0.0).astype(BF16)
            hs = jnp.dot(onehot, h_ref[...], preferred_element_type=F32).astype(BF16)
            a = _silu(jnp.dot(hs, wg_ref[...], preferred_element_type=F32)) * jnp.dot(hs, wu_ref[...],
                                                                                     preferred_element_type=F32)
            y = jnp.dot(a.astype(BF16), wd_ref[...], preferred_element_type=F32)
            w_row = jnp.sum(jnp.where(hit, comb_e, 0.0), axis=1, keepdims=True)
            acc_ref[...] += lax.dot_general(onehot, (y * w_row).astype(BF16), (((0,), (0,)), ((), ())),
                                            preferred_element_type=F32)

    @pl.when(e == pl.num_programs(1) - 1)
    def _():
        o_ref[...] = x_ref[...] + g2_ref[...] * acc_ref[...]


def _moe_ffn(x, mods, layer, g, wr, br, wg, wu, wd):
    tm = MOE_TM
    ew = lambda: pl.BlockSpec((None, D_MODEL, FF_EXPERT), lambda i, e: (e, 0, 0))
    before = jnp.triu(jnp.ones((tm, tm), BF16), k=1)
    return pl.pallas_call(
        _moe_kernel,
        grid=(N_TOK // tm, N_EXPERTS),
        in_specs=[pl.BlockSpec((tm, D_MODEL), lambda i, e: (i, 0)),
                  _mod_spec(layer, 3, tm), _mod_spec(layer, 4, tm), _mod_spec(layer, 5, tm),
                  pl.BlockSpec((1, D_MODEL), lambda i, e: (0, 0)),
                  pl.BlockSpec((ROUTE_ROWS, D_MODEL), lambda i, e: (0, 0)),
                  pl.BlockSpec((ROUTE_ROWS, LANE), lambda i, e: (0, 0)),
                  pl.BlockSpec((tm, tm), lambda i, e: (0, 0)),
                  ew(), ew(), ew()],
        out_specs=pl.BlockSpec((tm, D_MODEL), lambda i, e: (i, 0)),
        out_shape=jax.ShapeDtypeStruct((N_TOK, D_MODEL), F32),
        scratch_shapes=[pltpu.VMEM((tm, D_MODEL), BF16), pltpu.VMEM((2, ROUTE_ROWS, tm), F32),
                        pltpu.VMEM((tm, D_MODEL), F32)],
        compiler_params=_cparams("arbitrary", "arbitrary"),
        name="moe_ffn",
    )(x, mods, mods, mods, g, wr, br, before, wg, wu, wd)


def _reorder_w_in(w):
    pad = jnp.zeros((w.shape[0], PROJ_W - (COL_AB + N_GB)), w.dtype)
    return jnp.concatenate([w[:, 0:1536], w[:, 2304:3840], w[:, 1536:2048], w[:, 3840:4352], w[:, 4368:7440],
                            w[:, 2048:2304], w[:, 4352:4368], pad], axis=1).astype(BF16)


def _pad_lanes(v, fill=0.0):
    v = v.reshape(1, -1)
    return jnp.concatenate([v, jnp.full((1, LANE - v.shape[1]), fill, v.dtype)], axis=1)


def kernel(x_prompt, x_sample, cache_k, cache_v, state_fwd, state_bwd, c, c_ctx, norm1_g, norm2_g, w_ada, b_ada, w_in, conv_a_w, q_norm_g, k_norm_g, attn_sink, dn_conv_w, dn_a_log, dn_dt_bias, dn_norm_g, w_out_a, w_out_b, w_out_c, w_o, w_ff_gate, w_ff_up, w_ff_down, w_router, b_router, w_moe_gate, w_moe_up, w_moe_down):
    x = jnp.concatenate([x_prompt.reshape(P_TOK, D_MODEL), x_sample.reshape(S_TOK, D_MODEL)], axis=0)
    cmat = jnp.concatenate([c_ctx[None, :], c, jnp.zeros((SUBLANE - 1 - DEC_BATCH, D_MODEL), F32)], axis=0)
    mods = _modulations(cmat, w_ada, b_ada)
    cos_t, sin_t = _rope_tables()
    ck = cache_k.reshape(DEC_BATCH, DEPTH, PAST_LEN, LANE)
    cv = cache_v.reshape(DEC_BATCH, DEPTH, PAST_LEN, LANE)
    zero_state = jnp.zeros((BATCH, DN_HEADS, DN_DK, DN_DV), F32)

    ks_out, vs_out, sf_out, sb_out = [], [], [], []
    for l in range(DEPTH):
        proj = _input_projection(x, mods, l, norm1_g[l][None, :], _reorder_w_in(w_in[l]))

        qs, kx, vx, k_norm, v_raw = _qk_prep(proj, cos_t, sin_t, jnp.tile(q_norm_g[l], 2)[None, :],
                                             jnp.tile(k_norm_g[l], 2)[None, :])
        y_b = jnp.concatenate([_context_attention(attn_sink[l], qs, kx, vx),
                               _latent_attention(attn_sink[l], qs, kx, vx, ck, cv, l)], axis=0)
        ks_out.append(k_norm[:P_TOK].reshape(BATCH, SEQ, KV_HEADS, HEAD_DIM))
        vs_out.append(v_raw[:P_TOK].reshape(BATCH, SEQ, KV_HEADS, HEAD_DIM))

        local = _dn_local(proj, dn_conv_w[l], _pad_lanes(dn_a_log[l]), _pad_lanes(dn_dt_bias[l]))
        s0f = jnp.concatenate([zero_state, state_fwd[:, l]], axis=0)
        s0b = jnp.concatenate([zero_state, state_bwd[:, l]], axis=0)
        o_f, o_b, s_f, s_b = _dn_scan(*local, s0f, s0b)
        sf_out.append(s_f[:BATCH])
        sb_out.append(s_b[:BATCH])

        x = _merge(proj, y_b, o_f, o_b, x, mods, l, conv_a_w[l], dn_norm_g[l][None, :],
                   w_out_a[l].astype(BF16), w_out_b[l].astype(BF16), w_out_c[l].astype(BF16), w_o[l].astype(BF16))

        i = l // 2
        if l % 2 == 0:
            x = _dense_ffn(x, mods, l, norm2_g[l][None, :], w_ff_gate[i].astype(BF16), w_ff_up[i].astype(BF16),
                           w_ff_down[i].astype(BF16))
        else:
            wr = jnp.concatenate([w_router[i].T, jnp.zeros((ROUTE_ROWS - N_EXPERTS, D_MODEL), F32)], axis=0)
            br = jnp.concatenate([b_router[i], jnp.full((ROUTE_ROWS - N_EXPERTS,), -jnp.inf, F32)])
            x = _moe_ffn(x, mods, l, norm2_g[l][None, :], wr, jnp.broadcast_to(br[:, None], (ROUTE_ROWS, LANE)),
                         w_moe_gate[i].astype(BF16), w_moe_up[i].astype(BF16), w_moe_down[i].astype(BF16))

    return (x[:P_TOK].reshape(BATCH, SEQ, D_MODEL), x[P_TOK:].reshape(DEC_BATCH, DEC_SEQ, D_MODEL),
            jnp.stack(ks_out, axis=1), jnp.stack(vs_out, axis=1), jnp.stack(sf_out, axis=1), jnp.stack(sb_out, axis=1))
```

```python
import functools

import jax
import jax.numpy as jnp
from jax import lax
from jax.experimental import pallas as pl
from jax.experimental.pallas import tpu as pltpu

F32 = jnp.float32
BF16 = jnp.bfloat16

D_MODEL = 1024
BATCH = 32
SEQ = 256
DEPTH = 4
DEC_BATCH = 4
DEC_SEQ = 4096
PAST_LEN = 256
GRID_W = 64
A_WIDTH = 512
HEAD_DIM = 64
H_ATT = 8
KV_HEADS = 2
ROPE_BASE = 10000.0
DN_HEADS = 4
DN_DK = 128
DN_DV = 128
DN_CHUNK = 64
FF_DENSE = 2816
N_EXPERTS = 8
FF_EXPERT = 1024
N_MOD = 6
EPS = 1e-6
NEG = -1e30

P_TOK = BATCH * SEQ
S_TOK = DEC_BATCH * DEC_SEQ
N_TOK = P_TOK + S_TOK
N_SEQ = BATCH + DEC_BATCH
LANE = 128
SUBLANE = 8

COL_A = 0
COL_DQKV = 1536
COL_Q = 3072
COL_DZ = 3584
COL_GATES = 4096
COL_KV = 7168
COL_AB = 7424
PROJ_W = 7680

VMEM_LIMIT = 56 * 1024 * 1024


def _cparams(*sem):
    return pltpu.CompilerParams(dimension_semantics=sem, vmem_limit_bytes=VMEM_LIMIT)


def _group_of_tile(i, tm):
    p_tiles = P_TOK // tm
    return jnp.where(i < p_tiles, 0, 1 + (i - p_tiles) // (DEC_SEQ // tm))


def _mod_spec(layer, k, tm):
    return pl.BlockSpec((None, None, None, 1, D_MODEL), lambda i, *_: (layer, _group_of_tile(i, tm), k, 0, 0))


def _seq_pos(i, tm):
    row = i * tm + lax.broadcasted_iota(jnp.int32, (tm, 1), 0)
    seqlen = jnp.where(i * tm < P_TOK, SEQ, DEC_SEQ)
    return row & (seqlen - 1), seqlen


HALO = 16


def _halo_specs(tm, width, col_block):
    r = tm // HALO
    last = N_TOK // HALO - 1
    prev = pl.BlockSpec((HALO, width), lambda i, *_: (jnp.maximum(i * r - 1, 0), col_block))
    nxt = pl.BlockSpec((HALO, width), lambda i, *_: (jnp.minimum((i + 1) * r, last), col_block))
    return prev, nxt


def _dwconv3(x, prev_row, next_row, w, pos, seqlen):
    tm = x.shape[0]
    r = lax.broadcasted_iota(jnp.int32, (tm, 1), 0)
    xp = jnp.where(r == 0, prev_row, pltpu.roll(x, 1, 0))
    xn = jnp.where(r == tm - 1, next_row, pltpu.roll(x, tm - 1, 0))
    xp = jnp.where(pos == 0, 0.0, xp)
    xn = jnp.where(pos == seqlen - 1, 0.0, xn)
    return xp * w[0:1, :] + x * w[1:2, :] + xn * w[2:3, :]


def _silu(x):
    return x * jax.nn.sigmoid(x)


def _modulated_norm(x, g, shift, scale):
    y = x * lax.rsqrt(jnp.mean(x * x, axis=-1, keepdims=True) + EPS)
    return (y * g) * (1.0 + scale) + shift


def _mod_kernel(c_ref, w_ref, b_ref, o_ref):
    s = _silu(c_ref[...])
    o_ref[...] = jnp.dot(s.astype(BF16), w_ref[...].astype(BF16), preferred_element_type=F32) + b_ref[...]


def _modulations(cmat, w_ada, b_ada):
    tn = 1024
    out = pl.pallas_call(
        _mod_kernel,
        grid=(DEPTH, N_MOD * D_MODEL // tn),
        in_specs=[pl.BlockSpec((SUBLANE, D_MODEL), lambda l, j: (0, 0)),
                  pl.BlockSpec((None, D_MODEL, tn), lambda l, j: (l, 0, j)),
                  pl.BlockSpec((None, 1, tn), lambda l, j: (l, 0, j))],
        out_specs=pl.BlockSpec((None, SUBLANE, tn), lambda l, j: (l, 0, j)),
        out_shape=jax.ShapeDtypeStruct((DEPTH, SUBLANE, N_MOD * D_MODEL), F32),
        compiler_params=_cparams("arbitrary", "arbitrary"),
        name="ada_modulation",
    )(cmat, w_ada, b_ada.reshape(DEPTH, 1, N_MOD * D_MODEL))
    return out.reshape(DEPTH, SUBLANE, N_MOD, 1, D_MODEL)


PROJ_CHUNK = 1536


def _proj_kernel(x_ref, sh_ref, sc_ref, g_ref, w_ref, o_ref):
    h = _modulated_norm(x_ref[...], g_ref[...], sh_ref[...], sc_ref[...]).astype(BF16)
    for lo in range(0, PROJ_W, PROJ_CHUNK):
        o_ref[:, lo:lo + PROJ_CHUNK] = jnp.dot(h, w_ref[:, lo:lo + PROJ_CHUNK],
                                               preferred_element_type=F32).astype(o_ref.dtype)


def _input_projection(x, mods, layer, g, w):
    tm = 512
    w_spec = pl.BlockSpec((D_MODEL, PROJ_W), lambda i: (0, 0), pipeline_mode=pl.Buffered(1))
    return pl.pallas_call(
        _proj_kernel,
        grid=(N_TOK // tm,),
        in_specs=[pl.BlockSpec((tm, D_MODEL), lambda i: (i, 0)),
                  _mod_spec(layer, 0, tm), _mod_spec(layer, 1, tm),
                  pl.BlockSpec((1, D_MODEL), lambda i: (0, 0)),
                  w_spec],
        out_specs=pl.BlockSpec((tm, PROJ_W), lambda i: (i, 0)),
        out_shape=jax.ShapeDtypeStruct((N_TOK, PROJ_W), BF16),
        compiler_params=_cparams("arbitrary"),
        name="input_projection",
    )(x, mods, mods, g, w)


def _pair_norm(x, gain):
    lane = lax.broadcasted_iota(jnp.int32, x.shape, 1)
    x2 = x * x
    lo = jnp.sum(jnp.where(lane < HEAD_DIM, x2, 0.0), axis=-1, keepdims=True)
    hi = jnp.sum(jnp.where(lane >= HEAD_DIM, x2, 0.0), axis=-1, keepdims=True)
    inv = jnp.where(lane < HEAD_DIM, lax.rsqrt(lo / HEAD_DIM + EPS), lax.rsqrt(hi / HEAD_DIM + EPS))
    return x * inv * gain


def _pair_rope(x, cos, sin):
    lane = lax.broadcasted_iota(jnp.int32, x.shape, 1)
    swapped = jnp.where((lane & 1) == 0, pltpu.roll(x, LANE - 1, 1), pltpu.roll(x, 1, 1))
    return x * cos + swapped * sin


def _lane_halves(x):
    lane = lax.broadcasted_iota(jnp.int32, x.shape, 1)
    xr = pltpu.roll(x, HEAD_DIM, 1)
    lo, hi = lane < HEAD_DIM, lane >= HEAD_DIM
    return (jnp.where(lo, x, 0.0), jnp.where(hi, xr, 0.0), jnp.where(lo, xr, 0.0), jnp.where(hi, x, 0.0))


def _qk_prep_kernel(q_ref, kv_ref, cos_ref, sin_ref, qg_ref, kg_ref, qo_ref, ko_ref, vo_ref, kn_ref, v_ref):
    cos, sin = cos_ref[...], sin_ref[...]
    for p in range(H_ATT // 2):
        qn = _pair_norm(q_ref[:, p * LANE:(p + 1) * LANE].astype(F32), qg_ref[...])
        qo_ref[:, p * LANE:(p + 1) * LANE] = (_pair_rope(qn, cos, sin) * (HEAD_DIM ** -0.5)).astype(BF16)
    k = _pair_norm(kv_ref[:, 0:LANE].astype(F32), kg_ref[...])
    v = kv_ref[:, LANE:2 * LANE].astype(F32)
    kn_ref[...] = k
    v_ref[...] = v
    for j, (kx, vx) in enumerate(zip(_lane_halves(_pair_rope(k, cos, sin)), _lane_halves(v))):
        ko_ref[:, j * LANE:(j + 1) * LANE] = kx.astype(BF16)
        vo_ref[:, j * LANE:(j + 1) * LANE] = vx.astype(BF16)


def _qk_prep(proj, cos_t, sin_t, qg, kg):
    tm = 256
    tab = lambda i: jnp.where(i < P_TOK // tm, 0, 1 + (i - P_TOK // tm) % (DEC_SEQ // tm))
    wide = jax.ShapeDtypeStruct((N_TOK, 4 * LANE), BF16)
    slab = jax.ShapeDtypeStruct((N_TOK, LANE), F32)
    return pl.pallas_call(
        _qk_prep_kernel,
        grid=(N_TOK // tm,),
        in_specs=[pl.BlockSpec((tm, 4 * LANE), lambda i: (i, COL_Q // (4 * LANE))),
                  pl.BlockSpec((tm, 2 * LANE), lambda i: (i, COL_KV // (2 * LANE))),
                  pl.BlockSpec((tm, LANE), lambda i: (tab(i), 0)),
                  pl.BlockSpec((tm, LANE), lambda i: (tab(i), 0)),
                  pl.BlockSpec((1, LANE), lambda i: (0, 0)),
                  pl.BlockSpec((1, LANE), lambda i: (0, 0))],
        out_specs=[pl.BlockSpec((tm, 4 * LANE), lambda i: (i, 0)),
                   pl.BlockSpec((tm, 4 * LANE), lambda i: (i, 0)),
                   pl.BlockSpec((tm, 4 * LANE), lambda i: (i, 0)),
                   pl.BlockSpec((tm, LANE), lambda i: (i, 0)),
                   pl.BlockSpec((tm, LANE), lambda i: (i, 0))],
        out_shape=[wide, wide, wide, slab, slab],
        compiler_params=_cparams("arbitrary"),
        name="qk_prep",
    )(proj, proj, cos_t, sin_t, qg, kg)


def _rope_tables():
    pos = jnp.arange(DEC_SEQ)
    row = (pos // GRID_W).astype(F32)
    col = (pos % GRID_W).astype(F32)
    n_freq = HEAD_DIM // 4
    inv_freq = ROPE_BASE ** (-jnp.arange(n_freq, dtype=F32) / n_freq)
    ang = jnp.concatenate([row[:, None] * inv_freq, col[:, None] * inv_freq], axis=-1)
    cos = jnp.repeat(jnp.cos(ang), 2, axis=-1)
    sin = jnp.stack([-jnp.sin(ang), jnp.sin(ang)], axis=-1).reshape(DEC_SEQ, HEAD_DIM)
    cos = jnp.concatenate([jnp.ones((SEQ, HEAD_DIM), F32), cos], axis=0)
    sin = jnp.concatenate([jnp.zeros((SEQ, HEAD_DIM), F32), sin], axis=0)
    return jnp.tile(cos, (1, 2)), jnp.tile(sin, (1, 2))


def _attend_kv_head(q2, kx, vx, sink_a, sink_b, mask, kvh):
    nq = q2.shape[0] // 2
    row = lax.broadcasted_iota(jnp.int32, (2 * nq, 1), 0)
    out = None
    for half in range(2):
        j = 2 * kvh + half
        s = lax.dot_general(q2, kx[:, j * LANE:(j + 1) * LANE], (((1,), (1,)), ((), ())), preferred_element_type=F32)
        if mask is not None:
            s = jnp.where(mask, s, NEG)
        sink = jnp.where(row < nq, sink_a[half], sink_b[half])
        m = jnp.maximum(jnp.max(s, axis=-1, keepdims=True), sink)
        e = jnp.exp(s - m)
        denom = jnp.sum(e, axis=-1, keepdims=True) + jnp.exp(sink - m)
        o = jnp.dot(e.astype(BF16), vx[:, j * LANE:(j + 1) * LANE], preferred_element_type=F32) / denom
        out = o if out is None else out + o
    return out


def _ctx_attn_kernel(sink_ref, q_ref, k_ref, v_ref, o_ref):
    kx, vx = k_ref[...], v_ref[...]
    for kvh in range(KV_HEADS):
        p0, p1 = 2 * kvh, 2 * kvh + 1
        q2 = jnp.concatenate([q_ref[:, p0 * LANE:(p0 + 1) * LANE], q_ref[:, p1 * LANE:(p1 + 1) * LANE]], axis=0)
        sa = (sink_ref[2 * p0], sink_ref[2 * p0 + 1])
        sb = (sink_ref[2 * p1], sink_ref[2 * p1 + 1])
        o = _attend_kv_head(q2, kx, vx, sa, sb, None, kvh)
        o_ref[:, p0 * LANE:(p0 + 1) * LANE] = o[:SEQ]
        o_ref[:, p1 * LANE:(p1 + 1) * LANE] = o[SEQ:]


def _context_attention(sink, qs, kx, vx):
    blk = lambda: pl.BlockSpec((SEQ, 4 * LANE), lambda b: (b, 0))
    return pl.pallas_call(
        _ctx_attn_kernel,
        grid=(BATCH,),
        in_specs=[pl.BlockSpec(memory_space=pltpu.SMEM), blk(), blk(), blk()],
        out_specs=blk(),
        out_shape=jax.ShapeDtypeStruct((P_TOK, 4 * LANE), F32),
        compiler_params=_cparams("arbitrary"),
        name="context_attention",
    )(sink, qs, kx, vx)


LAT_BLOCK = 128


def _lat_attn_kernel(sink_ref, q_ref, kp_ref, kc_ref, kn_ref, vp_ref, vc_ref, vn_ref, ck_ref, cv_ref, o_ref):
    n = pl.program_id(1)
    nb = pl.num_programs(1)
    ck = jnp.concatenate(_lane_halves(ck_ref[...]), axis=1).astype(BF16)
    cv = jnp.concatenate(_lane_halves(cv_ref[...]), axis=1).astype(BF16)
    kx = jnp.concatenate([kp_ref[...], kc_ref[...], kn_ref[...], ck], axis=0)
    vx = jnp.concatenate([vp_ref[...], vc_ref[...], vn_ref[...], cv], axis=0)
    n_keys = 3 * LAT_BLOCK + PAST_LEN
    qi = lax.broadcasted_iota(jnp.int32, (2 * LAT_BLOCK, n_keys), 0) & (LAT_BLOCK - 1)
    kj = lax.broadcasted_iota(jnp.int32, (2 * LAT_BLOCK, n_keys), 1)
    far = 2 * LAT_BLOCK
    off_prev = jnp.where(n > 0, 0, far)
    off_next = jnp.where(n < nb - 1, 0, far)
    t_prev = jnp.where(kj < LAT_BLOCK, kj - qi - off_prev, 0)
    t_next = jnp.where(kj >= 2 * LAT_BLOCK, jnp.where(kj < 3 * LAT_BLOCK, qi - off_next - (kj - 2 * LAT_BLOCK), 0), 0)
    mask = jnp.minimum(t_prev, t_next) >= 0
    for kvh in range(KV_HEADS):
        p0, p1 = 2 * kvh, 2 * kvh + 1
        q2 = jnp.concatenate([q_ref[:, p0 * LANE:(p0 + 1) * LANE], q_ref[:, p1 * LANE:(p1 + 1) * LANE]], axis=0)
        sa = (sink_ref[2 * p0], sink_ref[2 * p0 + 1])
        sb = (sink_ref[2 * p1], sink_ref[2 * p1 + 1])
        o = _attend_kv_head(q2, kx, vx, sa, sb, mask, kvh)
        o_ref[:, p0 * LANE:(p0 + 1) * LANE] = o[:LAT_BLOCK]
        o_ref[:, p1 * LANE:(p1 + 1) * LANE] = o[LAT_BLOCK:]


def _latent_attention(sink, qs, kx, vx, cache_k, cache_v, layer):
    nb = DEC_SEQ // LAT_BLOCK
    base = P_TOK // LAT_BLOCK
    cur = lambda b, n: (base + b * nb + n, 0)
    prev = lambda b, n: (base + b * nb + jnp.maximum(n - 1, 0), 0)
    nxt = lambda b, n: (base + b * nb + jnp.minimum(n + 1, nb - 1), 0)
    blk = lambda f: pl.BlockSpec((LAT_BLOCK, 4 * LANE), f)
    ctx = pl.BlockSpec((None, None, PAST_LEN, LANE), lambda b, n: (b, layer, 0, 0))
    return pl.pallas_call(
        _lat_attn_kernel,
        grid=(DEC_BATCH, nb),
        in_specs=[pl.BlockSpec(memory_space=pltpu.SMEM), blk(cur),
                  blk(prev), blk(cur), blk(nxt), blk(prev), blk(cur), blk(nxt), ctx, ctx],
        out_specs=pl.BlockSpec((LAT_BLOCK, 4 * LANE), lambda b, n: (b * nb + n, 0)),
        out_shape=jax.ShapeDtypeStruct((S_TOK, 4 * LANE), F32),
        compiler_params=_cparams("arbitrary", "arbitrary"),
        name="latent_attention",
    )(sink, qs, kx, kx, kx, vx, vx, vx, cache_k, cache_v)


DN_W = DN_HEADS * DN_DK
N_GB = 2 * 2 * DN_HEADS
DN_ROWS = DN_HEADS * DN_CHUNK
DN_TILE = SEQ
CH_PER_TILE = DN_TILE // DN_CHUNK
SEQ_GROUP = DEC_BATCH
P_GC = (BATCH // SEQ_GROUP) * (SEQ // DN_CHUNK)
N_GC = P_GC + DEC_SEQ // DN_CHUNK


def _dot_hi(a, b):
    return jnp.dot(a, b, precision=lax.Precision.HIGHEST, preferred_element_type=F32)


def _dot_nt(a, b):
    return lax.dot_general(a.astype(BF16), b.astype(BF16), (((1,), (1,)), ((), ())), preferred_element_type=F32)


def _dot_tn(a, b):
    return lax.dot_general(a.astype(BF16), b.astype(BF16), (((0,), (0,)), ((), ())), preferred_element_type=F32)


def _dot_bf(a, b):
    return jnp.dot(a.astype(BF16), b.astype(BF16), preferred_element_type=F32)


def _stack_heads(x):
    return jnp.concatenate([x[:, h * DN_DK:(h + 1) * DN_DK] for h in range(DN_HEADS)], axis=0)


def _stack_cols(m, c0):
    return jnp.concatenate([m[:, c0 + h:c0 + h + 1] for h in range(DN_HEADS)], axis=0)


def _head(x, h):
    return x[h * DN_CHUNK:(h + 1) * DN_CHUNK]


def _per_head(f):
    return jnp.concatenate([f(h) for h in range(DN_HEADS)], axis=0)


def _tri_inverses(lms, xor):
    eye = (xor == 0).astype(F32)
    ts = [eye - jnp.where(xor < 2, lm, 0.0) for lm in lms]
    k = 1
    while (2 << k) <= DN_CHUNK:
        level = (xor >> k) == 1
        ms = [jnp.where(level, lm, 0.0).astype(BF16) for lm in lms]
        tbs = [t.astype(BF16) for t in ts]
        mts = [_per_head(lambda h: _dot_bf(_head(m, h), _head(tb, h))).astype(BF16) for m, tb in zip(ms, tbs)]
        ts = [t - _per_head(lambda h: _dot_bf(_head(tb, h), _head(mt, h))) for t, tb, mt in zip(ts, tbs, mts)]
        k += 1
    return ts


def _dn_tile_slot(i):
    p_tiles = P_TOK // DN_TILE
    tiles_per_seq = DEC_SEQ // DN_TILE
    j = i - p_tiles
    blk = jnp.where(i < p_tiles, i // SEQ_GROUP, P_GC // CH_PER_TILE + j % tiles_per_seq)
    q = jnp.where(i < p_tiles, i % SEQ_GROUP, j // tiles_per_seq)
    return blk, q


def _dn_local_kernel(x_ref, xp_ref, xn_ref, ab_ref, w_ref, alog_ref, dtb_ref,
                     u_ref, w_out_ref, qd_ref, kd_ref, qk_ref, egl_ref):
    pos, seqlen = _seq_pos(pl.program_id(0), DN_TILE)
    s = _silu(_dwconv3(x_ref[...].astype(F32), xp_ref[...].astype(F32)[HALO - 1:HALO, :],
                       xn_ref[...].astype(F32)[0:1, :], w_ref[...], pos, seqlen))
    ab = ab_ref[...].astype(F32)
    lane = lax.broadcasted_iota(jnp.int32, ab.shape, 1)
    z = ab + dtb_ref[...]
    softplus = jnp.maximum(z, 0.0) + jnp.log1p(jnp.exp(-jnp.abs(z)))
    gb = jnp.where(lane < N_GB // 2, -jnp.exp(alog_ref[...]) * softplus, jax.nn.sigmoid(ab))

    c, r = DN_CHUNK, DN_ROWS
    ri = lax.broadcasted_iota(jnp.int32, (r, c), 0) & (c - 1)
    ci = lax.broadcasted_iota(jnp.int32, (r, c), 1)
    xor = ri ^ ci
    masks = ((ri >= ci, ri > ci), (ri <= ci, ri < ci))
    tri = (lax.broadcasted_iota(jnp.int32, (c, c), 0) >= lax.broadcasted_iota(jnp.int32, (c, c), 1)).astype(F32)

    problems, lms = [], []
    for ch in range(CH_PER_TILE):
        sc = s[ch * c:(ch + 1) * c]
        qst = _stack_heads(sc[:, 0:DN_W])
        kst = _stack_heads(sc[:, DN_W:2 * DN_W])
        vst = _stack_heads(sc[:, 2 * DN_W:3 * DN_W])
        qst = qst * lax.rsqrt(jnp.sum(qst * qst, axis=-1, keepdims=True) + EPS) * (DN_DK ** -0.5)
        kst = kst * lax.rsqrt(jnp.sum(kst * kst, axis=-1, keepdims=True) + EPS)
        gbc = gb[ch * c:(ch + 1) * c]
        prefix = _dot_hi(tri, gbc)
        suffix = prefix[c - 1:c] - prefix + gbc
        qb, kb = qst.astype(BF16), kst.astype(BF16)
        gram = _per_head(lambda h: _dot_nt(_head(kb, h), _head(kb, h)))
        qk0 = _per_head(lambda h: _dot_nt(_head(qb, h), _head(kb, h)))
        for d, (gc, last) in enumerate(((prefix, c - 1), (suffix, 0))):
            incl, strict = masks[d]
            c0 = d * DN_HEADS
            gcol = _stack_cols(gc, c0)
            beta = _stack_cols(gbc, N_GB // 2 + c0)
            gct = gc.T
            grow = _per_head(lambda h: jnp.broadcast_to(gct[c0 + h:c0 + h + 1, :], (c, c)))
            decay = jnp.exp(jnp.where(incl, gcol - grow, NEG))
            lms.append(jnp.where(strict, gram * beta * decay, 0.0))
            eg = jnp.exp(gcol)
            glast = _per_head(lambda h: jnp.broadcast_to(gc[last:last + 1, c0 + h:c0 + h + 1], (c, 1)))
            qd_ref[ch, d] = (qst * eg).astype(BF16)
            kd_ref[ch, d] = (kst * jnp.exp(glast - gcol)).astype(BF16)
            qk_ref[ch, d] = jnp.where(incl, qk0 * decay, 0.0).astype(BF16)
            egl_ref[ch, d] = jnp.concatenate([jnp.broadcast_to(jnp.exp(gc[last:last + 1, c0 + h:c0 + h + 1]), (1, DN_DV))
                                              for h in range(DN_HEADS)], axis=1)
            problems.append((ch, d, jnp.concatenate([vst * beta, kst * (beta * eg)], axis=1).astype(BF16)))

    for (ch, d, rhs), t in zip(problems, _tri_inverses(lms, xor)):
        tb = t.astype(BF16)
        x = _per_head(lambda h: jnp.dot(_head(tb, h), _head(rhs, h), preferred_element_type=F32))
        u_ref[ch, d] = x[:, 0:DN_DV]
        w_out_ref[ch, d] = x[:, DN_DV:2 * DN_DV].astype(BF16)


def _dn_local(proj, conv_w, alog, dtb):
    wq = 3 * DN_W
    prev, nxt = _halo_specs(DN_TILE, wq, COL_DQKV // wq)

    def slot_spec(*tail):
        return pl.BlockSpec((CH_PER_TILE, None, 2) + tail,
                            lambda i: _dn_tile_slot(i) + (0,) * (1 + len(tail)))

    def slot_shape(dtype, *tail):
        return jax.ShapeDtypeStruct((N_GC, SEQ_GROUP, 2) + tail, dtype)

    return pl.pallas_call(
        _dn_local_kernel,
        grid=(N_TOK // DN_TILE,),
        in_specs=[pl.BlockSpec((DN_TILE, wq), lambda i: (i, COL_DQKV // wq)), prev, nxt,
                  pl.BlockSpec((DN_TILE, LANE), lambda i: (i, COL_AB // LANE)),
                  pl.BlockSpec((3, wq), lambda i: (0, 0)),
                  pl.BlockSpec((1, LANE), lambda i: (0, 0)),
                  pl.BlockSpec((1, LANE), lambda i: (0, 0))],
        out_specs=[slot_spec(DN_ROWS, DN_DV), slot_spec(DN_ROWS, DN_DK), slot_spec(DN_ROWS, DN_DK),
                   slot_spec(DN_ROWS, DN_DK), slot_spec(DN_ROWS, DN_CHUNK), slot_spec(1, DN_W)],
        out_shape=[slot_shape(F32, DN_ROWS, DN_DV), slot_shape(BF16, DN_ROWS, DN_DK), slot_shape(BF16, DN_ROWS, DN_DK),
                   slot_shape(BF16, DN_ROWS, DN_DK), slot_shape(BF16, DN_ROWS, DN_CHUNK), slot_shape(F32, 1, DN_W)],
        compiler_params=_cparams("arbitrary"),
        name="deltanet_local",
    )(proj, proj, proj, proj, conv_w, alog, dtb)


def _scan_step(t):
    pc, sc = SEQ // DN_CHUNK, DEC_SEQ // DN_CHUNK
    is_p = t < P_GC
    g = jnp.where(is_p, t // pc, BATCH // SEQ_GROUP)
    s = jnp.where(is_p, t % pc, t - P_GC)
    n = jnp.where(is_p, pc, sc)
    return g, s, n, t, t - s + (n - 1 - s)


def _dn_scan_kernel(uf_ref, wf_ref, qdf_ref, kdf_ref, qkf_ref, egf_ref, ub_ref, wb_ref, qdb_ref, kdb_ref, qkb_ref,
                    egb_ref, s0f_ref, s0b_ref, of_ref, ob_ref, sf_ref, sb_ref, state_ref):
    _, s, n, _, _ = _scan_step(pl.program_id(0))
    c, r = DN_CHUNK, DN_ROWS
    heads = range(DN_HEADS)

    @pl.when(s == 0)
    def _():
        for q in range(SEQ_GROUP):
            state_ref[0, q] = jnp.concatenate([s0f_ref[q, h] for h in heads], axis=1)
            state_ref[1, q] = jnp.concatenate([s0b_ref[q, h] for h in heads], axis=1)

    dirs = ((uf_ref, wf_ref, qdf_ref, kdf_ref, qkf_ref, egf_ref, of_ref),
            (ub_ref, wb_ref, qdb_ref, kdb_ref, qkb_ref, egb_ref, ob_ref))
    chains = [(d, q) + refs for d, refs in enumerate(dirs) for q in range(SEQ_GROUP)]
    states = [state_ref[d, q] for d, q, *_ in chains]
    def state_products(w, qd, st):
        sb = st.astype(BF16)
        both = [jnp.dot(jnp.concatenate([_head(w, h), _head(qd, h)], axis=0), sb[:, h * DN_DV:(h + 1) * DN_DV],
                        preferred_element_type=F32) for h in heads]
        return (jnp.concatenate([b[0:c] for b in both], axis=0), jnp.concatenate([b[c:2 * c] for b in both], axis=0))

    boths = [state_products(w_ref[q], qd_ref[q], st) for (d, q, u_ref, w_ref, qd_ref, *_), st in zip(chains, states)]
    v_news = [(u_ref[q] - ws).astype(BF16) for (d, q, u_ref, *_), (ws, _) in zip(chains, boths)]
    for (d, q, u_ref, w_ref, qd_ref, kd_ref, qk_ref, eg_ref, o_ref), st, (_, qs), v_new in zip(chains, states, boths, v_news):
        qk = qk_ref[q]
        o = qs + _per_head(lambda h: jnp.dot(_head(qk, h), _head(v_new, h), preferred_element_type=F32))
        kd = kd_ref[q]
        outer = jnp.concatenate([lax.dot_general(_head(kd, h), _head(v_new, h), (((0,), (0,)), ((), ())),
                                                 preferred_element_type=F32) for h in heads], axis=1)
        state_ref[d, q] = st * eg_ref[q] + outer
        o_ref[q] = jnp.concatenate([o[h * c:(h + 1) * c] for h in heads], axis=1)

    @pl.when(s == n - 1)
    def _():
        for q in range(SEQ_GROUP):
            for h in heads:
                sf_ref[q, h] = state_ref[0, q][:, h * DN_DV:(h + 1) * DN_DV]
                sb_ref[q, h] = state_ref[1, q][:, h * DN_DV:(h + 1) * DN_DV]


def _dn_scan(u, w, qd, kd, qk, egl, s0f, s0b):
    fwd = lambda t: _scan_step(t)[3]
    bwd = lambda t: _scan_step(t)[4]
    grp = lambda t: _scan_step(t)[0]

    def specs(slot, d):
        sp = lambda *tail: pl.BlockSpec((None, SEQ_GROUP, None) + tail, lambda t: (slot(t), 0, d) + (0,) * len(tail))
        return [sp(DN_ROWS, DN_DV), sp(DN_ROWS, DN_DK), sp(DN_ROWS, DN_DK), sp(DN_ROWS, DN_DK), sp(DN_ROWS, DN_CHUNK),
                sp(1, DN_W)]

    st = lambda: pl.BlockSpec((SEQ_GROUP, DN_HEADS, DN_DK, DN_DV), lambda t: (grp(t), 0, 0, 0))
    o_spec = lambda slot: pl.BlockSpec((None, SEQ_GROUP, DN_CHUNK, DN_W), lambda t: (slot(t), 0, 0, 0))
    o_shape = jax.ShapeDtypeStruct((N_GC, SEQ_GROUP, DN_CHUNK, DN_W), F32)
    st_shape = jax.ShapeDtypeStruct((N_SEQ, DN_HEADS, DN_DK, DN_DV), F32)
    args = (u, w, qd, kd, qk, egl)
    return pl.pallas_call(
        _dn_scan_kernel,
        grid=(N_GC,),
        in_specs=specs(fwd, 0) + specs(bwd, 1) + [st(), st()],
        out_specs=[o_spec(fwd), o_spec(bwd), st(), st()],
        out_shape=[o_shape, o_shape, st_shape, st_shape],
        scratch_shapes=[pltpu.VMEM((2, SEQ_GROUP, DN_DK, DN_W), F32)],
        compiler_params=_cparams("arbitrary"),
        name="deltanet_scan",
    )(*args, *args, s0f, s0b)


def _merge_kernel(a_ref, ap_ref, an_ref, yb_ref, of_ref, ob_ref, z_ref, ga_ref, gb_ref, gc_ref, x_ref, g1_ref,
                  cw_ref, ng_ref, wa_ref, wb_ref, wc_ref, wo_ref, o_ref):
    tm = x_ref.shape[0]
    pos, seqlen = _seq_pos(pl.program_id(0), tm)
    w = A_WIDTH
    a = a_ref[...].astype(F32)
    ap = ap_ref[...].astype(F32)[HALO - 1:HALO, :]
    an = an_ref[...].astype(F32)[0:1, :]
    u = a[:, w:2 * w] * a[:, 2 * w:3 * w]
    up = ap[:, w:2 * w] * ap[:, 2 * w:3 * w]
    un = an[:, w:2 * w] * an[:, 2 * w:3 * w]
    y_a = a[:, 0:w] * _dwconv3(u, up, un, cw_ref[...], pos, seqlen)
    merged = jax.nn.sigmoid(ga_ref[...].astype(F32)) * _dot_bf(y_a, wa_ref[...])
    merged += jax.nn.sigmoid(gb_ref[...].astype(F32)) * _dot_bf(yb_ref[...], wb_ref[...])
    o_sum = (of_ref[...] + ob_ref[...]).reshape(tm, DN_W)
    parts = []
    for h in range(DN_HEADS):
        sl = slice(h * DN_DV, (h + 1) * DN_DV)
        o = o_sum[:, sl]
        o = o * lax.rsqrt(jnp.mean(o * o, axis=-1, keepdims=True) + EPS) * ng_ref[...]
        parts.append(o * _silu(z_ref[:, sl].astype(F32)))
    y_c = jnp.concatenate(parts, axis=1)
    merged += jax.nn.sigmoid(gc_ref[...].astype(F32)) * _dot_bf(y_c, wc_ref[...])
    o_ref[...] = x_ref[...] + g1_ref[...] * _dot_bf(merged, wo_ref[...])


def _merge(proj, y_b, o_f, o_b, x, mods, layer, conv_a_w, dn_norm_g, wa, wb, wc, wo):
    tm = DN_TILE
    wa3 = 3 * A_WIDTH
    prev, nxt = _halo_specs(tm, wa3, COL_A // wa3)
    tok = lambda w, cb=0: pl.BlockSpec((tm, w), lambda i: (i, cb))
    full = lambda a: pl.BlockSpec(a.shape, lambda i: (0,) * a.ndim)
    scan_out = lambda: pl.BlockSpec((CH_PER_TILE, None, DN_CHUNK, DN_W), lambda i: _dn_tile_slot(i) + (0, 0))
    gate0 = COL_GATES // D_MODEL
    return pl.pallas_call(
        _merge_kernel,
        grid=(N_TOK // tm,),
        in_specs=[tok(wa3, COL_A // wa3), prev, nxt, tok(A_WIDTH), scan_out(), scan_out(), tok(DN_W, COL_DZ // DN_W),
                  tok(D_MODEL, gate0), tok(D_MODEL, gate0 + 1), tok(D_MODEL, gate0 + 2), tok(D_MODEL),
                  _mod_spec(layer, 2, tm), full(conv_a_w), full(dn_norm_g), full(wa), full(wb), full(wc), full(wo)],
        out_specs=tok(D_MODEL),
        out_shape=jax.ShapeDtypeStruct((N_TOK, D_MODEL), F32),
        compiler_params=_cparams("arbitrary"),
        name="mixer_merge",
    )(proj, proj, proj, y_b, o_f, o_b, proj, proj, proj, proj, x, mods, conv_a_w, dn_norm_g, wa, wb, wc, wo)


FF_CHUNK = 256


def _swiglu_chunked(h, wg_ref, wu_ref, wd_ref):
    width = wg_ref.shape[1]
    bounds = list(range(0, width, FF_CHUNK)) + [width]
    if bounds[-1] - bounds[-2] < FF_CHUNK and len(bounds) > 2:
        del bounds[-2]
    chunks = list(zip(bounds[:-1], bounds[1:]))
    gate_up = lambda lo, hi: (jnp.dot(h, wg_ref[:, lo:hi], preferred_element_type=F32),
                              jnp.dot(h, wu_ref[:, lo:hi], preferred_element_type=F32))
    out = None
    nxt = gate_up(*chunks[0])
    for c, (lo, hi) in enumerate(chunks):
        g, u = nxt
        if c + 1 < len(chunks):
            nxt = gate_up(*chunks[c + 1])
        y = jnp.dot((_silu(g) * u).astype(BF16), wd_ref[lo:hi, :], preferred_element_type=F32)
        out = y if out is None else out + y
    return out


def _ffn_kernel(x_ref, sh_ref, sc_ref, g2_ref, g_ref, wg_ref, wu_ref, wd_ref, o_ref, h_ref, acc_ref):
    f = pl.program_id(1)

    @pl.when(f == 0)
    def _():
        h_ref[...] = _modulated_norm(x_ref[...], g_ref[...], sh_ref[...], sc_ref[...]).astype(BF16)
        acc_ref[...] = jnp.zeros_like(acc_ref)

    acc_ref[...] += _swiglu_chunked(h_ref[...], wg_ref, wu_ref, wd_ref)

    @pl.when(f == pl.num_programs(1) - 1)
    def _():
        o_ref[...] = x_ref[...] + g2_ref[...] * acc_ref[...]


def _dense_ffn(x, mods, layer, g, wg, wu, wd):
    tm, tf = 1024, 1408
    return pl.pallas_call(
        _ffn_kernel,
        grid=(N_TOK // tm, FF_DENSE // tf),
        in_specs=[pl.BlockSpec((tm, D_MODEL), lambda i, f: (i, 0)),
                  _mod_spec(layer, 3, tm), _mod_spec(layer, 4, tm), _mod_spec(layer, 5, tm),
                  pl.BlockSpec((1, D_MODEL), lambda i, f: (0, 0)),
                  pl.BlockSpec((D_MODEL, tf), lambda i, f: (0, f)),
                  pl.BlockSpec((D_MODEL, tf), lambda i, f: (0, f)),
                  pl.BlockSpec((tf, D_MODEL), lambda i, f: (f, 0))],
        out_specs=pl.BlockSpec((tm, D_MODEL), lambda i, f: (i, 0)),
        out_shape=jax.ShapeDtypeStruct((N_TOK, D_MODEL), F32),
        scratch_shapes=[pltpu.VMEM((tm, D_MODEL), BF16), pltpu.VMEM((tm, D_MODEL), F32)],
        compiler_params=_cparams("arbitrary", "arbitrary"),
        name="dense_ffn",
    )(x, mods, mods, mods, g, wg, wu, wd)


MOE_TM = 1024
MOE_BLOCKS = ((0, 256), (256, 128), (384, 128), (512, 256), (768, 256))
ROUTE_ROWS = 16


def _moe_kernel(x_ref, sh_ref, sc_ref, g2_ref, g_ref, wr_ref, br_ref, before_ref, wg_ref, wu_ref, wd_ref, o_ref,
                h_ref, route_ref, acc_ref):
    e = pl.program_id(1)
    sub = lax.broadcasted_iota(jnp.int32, (ROUTE_ROWS, MOE_TM), 0)

    @pl.when(e == 0)
    def _():
        hb = _modulated_norm(x_ref[...], g_ref[...], sh_ref[...], sc_ref[...]).astype(BF16)
        h_ref[...] = hb
        logits = lax.dot_general(wr_ref[...].astype(BF16), hb, (((1,), (1,)), ((), ())),
                                 preferred_element_type=F32) + br_ref[:, 0:1]
        m1 = jnp.max(logits, axis=0, keepdims=True)
        i1 = jnp.min(jnp.where(logits == m1, sub, ROUTE_ROWS), axis=0, keepdims=True)
        rest = jnp.where(sub == i1, -jnp.inf, logits)
        m2 = jnp.max(rest, axis=0, keepdims=True)
        i2 = jnp.min(jnp.where(rest == m2, sub, ROUTE_ROWS), axis=0, keepdims=True)
        e2 = jnp.exp(m2 - m1)
        route_ref[0] = jnp.where(sub == i1, 1.0 / (1.0 + e2), 0.0) + jnp.where(sub == i2, e2 / (1.0 + e2), 0.0)
        sel = jnp.where(sub == i1, 1.0, jnp.where(sub == i2, 1.0, 0.0))
        rank = jnp.dot(sel.astype(BF16), before_ref[...], preferred_element_type=F32)
        route_ref[1] = jnp.where(sel > 0.5, rank, -1.0)
        acc_ref[...] = jnp.zeros_like(acc_ref)

    pick = sub == e
    comb_e = jnp.sum(jnp.where(pick, route_ref[0], 0.0), axis=0, keepdims=True)
    key_e = jnp.sum(jnp.where(pick, route_ref[1], 0.0), axis=0, keepdims=True)
    count = jnp.sum(jnp.where(key_e >= 0.0, 1.0, 0.0))

    for start, rows in MOE_BLOCKS:
        @pl.when(count > start)
        def _():
            r = (lax.broadcasted_iota(jnp.int32, (rows, MOE_TM), 0) + start).astype(F32)
            hit = key_e == r
            onehot = jnp.where(hit, 1.0, 0.0).astype(BF16)
            hs = jnp.dot(onehot, h_ref[...], preferred_element_type=F32).astype(BF16)
            y = _swiglu_chunked(hs, wg_ref, wu_ref, wd_ref)
            w_row = jnp.sum(jnp.where(hit, comb_e, 0.0), axis=1, keepdims=True)
            acc_ref[...] += lax.dot_general(onehot, (y * w_row).astype(BF16), (((0,), (0,)), ((), ())),
                                            preferred_element_type=F32)

    @pl.when(e == pl.num_programs(1) - 1)
    def _():
        o_ref[...] = x_ref[...] + g2_ref[...] * acc_ref[...]


def _moe_ffn(x, mods, layer, g, wr, br, wg, wu, wd):
    tm = MOE_TM
    ew = lambda: pl.BlockSpec((None, D_MODEL, FF_EXPERT), lambda i, e: (e, 0, 0))
    before = jnp.triu(jnp.ones((tm, tm), BF16), k=1)
    return pl.pallas_call(
        _moe_kernel,
        grid=(N_TOK // tm, N_EXPERTS),
        in_specs=[pl.BlockSpec((tm, D_MODEL), lambda i, e: (i, 0)),
                  _mod_spec(layer, 3, tm), _mod_spec(layer, 4, tm), _mod_spec(layer, 5, tm),
                  pl.BlockSpec((1, D_MODEL), lambda i, e: (0, 0)),
                  pl.BlockSpec((ROUTE_ROWS, D_MODEL), lambda i, e: (0, 0)),
                  pl.BlockSpec((ROUTE_ROWS, LANE), lambda i, e: (0, 0)),
                  pl.BlockSpec((tm, tm), lambda i, e: (0, 0)),
                  ew(), ew(), ew()],
        out_specs=pl.BlockSpec((tm, D_MODEL), lambda i, e: (i, 0)),
        out_shape=jax.ShapeDtypeStruct((N_TOK, D_MODEL), F32),
        scratch_shapes=[pltpu.VMEM((tm, D_MODEL), BF16), pltpu.VMEM((2, ROUTE_ROWS, tm), F32),
                        pltpu.VMEM((tm, D_MODEL), F32)],
        compiler_params=_cparams("arbitrary", "arbitrary"),
        name="moe_ffn",
    )(x, mods, mods, mods, g, wr, br, before, wg, wu, wd)


def _reorder_w_in(w):
    pad = jnp.zeros((w.shape[0], PROJ_W - (COL_AB + N_GB)), w.dtype)
    return jnp.concatenate([w[:, 0:1536], w[:, 2304:3840], w[:, 1536:2048], w[:, 3840:4352], w[:, 4368:7440],
                            w[:, 2048:2304], w[:, 4352:4368], pad], axis=1).astype(BF16)


def _pad_lanes(v, fill=0.0):
    v = v.reshape(1, -1)
    return jnp.concatenate([v, jnp.full((1, LANE - v.shape[1]), fill, v.dtype)], axis=1)


def kernel(x_prompt, x_sample, cache_k, cache_v, state_fwd, state_bwd, c, c_ctx, norm1_g, norm2_g, w_ada, b_ada, w_in, conv_a_w, q_norm_g, k_norm_g, attn_sink, dn_conv_w, dn_a_log, dn_dt_bias, dn_norm_g, w_out_a, w_out_b, w_out_c, w_o, w_ff_gate, w_ff_up, w_ff_down, w_router, b_router, w_moe_gate, w_moe_up, w_moe_down):
    x = jnp.concatenate([x_prompt.reshape(P_TOK, D_MODEL), x_sample.reshape(S_TOK, D_MODEL)], axis=0)
    cmat = jnp.concatenate([c_ctx[None, :], c, jnp.zeros((SUBLANE - 1 - DEC_BATCH, D_MODEL), F32)], axis=0)
    mods = _modulations(cmat, w_ada, b_ada)
    cos_t, sin_t = _rope_tables()
    ck = cache_k.reshape(DEC_BATCH, DEPTH, PAST_LEN, LANE)
    cv = cache_v.reshape(DEC_BATCH, DEPTH, PAST_LEN, LANE)
    zero_state = jnp.zeros((BATCH, DN_HEADS, DN_DK, DN_DV), F32)

    ks_out, vs_out, sf_out, sb_out = [], [], [], []
    for l in range(DEPTH):
        proj = _input_projection(x, mods, l, norm1_g[l][None, :], _reorder_w_in(w_in[l]))

        qs, kx, vx, k_norm, v_raw = _qk_prep(proj, cos_t, sin_t, jnp.tile(q_norm_g[l], 2)[None, :],
                                             jnp.tile(k_norm_g[l], 2)[None, :])
        y_b = jnp.concatenate([_context_attention(attn_sink[l], qs, kx, vx),
                               _latent_attention(attn_sink[l], qs, kx, vx, ck, cv, l)], axis=0)
        ks_out.append(k_norm[:P_TOK].reshape(BATCH, SEQ, KV_HEADS, HEAD_DIM))
        vs_out.append(v_raw[:P_TOK].reshape(BATCH, SEQ, KV_HEADS, HEAD_DIM))

        local = _dn_local(proj, dn_conv_w[l], _pad_lanes(dn_a_log[l]), _pad_lanes(dn_dt_bias[l]))
        s0f = jnp.concatenate([zero_state, state_fwd[:, l]], axis=0)
        s0b = jnp.concatenate([zero_state, state_bwd[:, l]], axis=0)
        o_f, o_b, s_f, s_b = _dn_scan(*local, s0f, s0b)
        sf_out.append(s_f[:BATCH])
        sb_out.append(s_b[:BATCH])

        x = _merge(proj, y_b, o_f, o_b, x, mods, l, conv_a_w[l], dn_norm_g[l][None, :],
                   w_out_a[l].astype(BF16), w_out_b[l].astype(BF16), w_out_c[l].astype(BF16), w_o[l].astype(BF16))

        i = l // 2
        if l % 2 == 0:
            x = _dense_ffn(x, mods, l, norm2_g[l][None, :], w_ff_gate[i].astype(BF16), w_ff_up[i].astype(BF16),
                           w_ff_down[i].astype(BF16))
        else:
            wr = jnp.concatenate([w_router[i].T, jnp.zeros((ROUTE_ROWS - N_EXPERTS, D_MODEL), F32)], axis=0)
            br = jnp.concatenate([b_router[i], jnp.full((ROUTE_ROWS - N_EXPERTS,), -jnp.inf, F32)])
            x = _moe_ffn(x, mods, l, norm2_g[l][None, :], wr, jnp.broadcast_to(br[:, None], (ROUTE_ROWS, LANE)),
                         w_moe_gate[i].astype(BF16), w_moe_up[i].astype(BF16), w_moe_down[i].astype(BF16))

    return (x[:P_TOK].reshape(BATCH, SEQ, D_MODEL), x[P_TOK:].reshape(DEC_BATCH, DEC_SEQ, D_MODEL),
            jnp.stack(ks_out, axis=1), jnp.stack(vs_out, axis=1), jnp.stack(sf_out, axis=1), jnp.stack(sb_out, axis=1))
```

```python
import functools

import jax
import jax.numpy as jnp
from jax import lax
from jax.experimental import pallas as pl
from jax.experimental.pallas import tpu as pltpu

F32 = jnp.float32
BF16 = jnp.bfloat16

D_MODEL = 1024
BATCH = 32
SEQ = 256
DEPTH = 4
DEC_BATCH = 4
DEC_SEQ = 4096
PAST_LEN = 256
GRID_W = 64
A_WIDTH = 512
HEAD_DIM = 64
H_ATT = 8
KV_HEADS = 2
ROPE_BASE = 10000.0
DN_HEADS = 4
DN_DK = 128
DN_DV = 128
DN_CHUNK = 64
FF_DENSE = 2816
N_EXPERTS = 8
FF_EXPERT = 1024
N_MOD = 6
EPS = 1e-6
NEG = -1e30

P_TOK = BATCH * SEQ
S_TOK = DEC_BATCH * DEC_SEQ
N_TOK = P_TOK + S_TOK
N_SEQ = BATCH + DEC_BATCH
LANE = 128
SUBLANE = 8

COL_A = 0
COL_DQKV = 1536
COL_Q = 3072
COL_DZ = 3584
COL_GATES = 4096
COL_KV = 7168
COL_AB = 7424
PROJ_W = 7680

VMEM_LIMIT = 56 * 1024 * 1024


def _cparams(*sem):
    return pltpu.CompilerParams(dimension_semantics=sem, vmem_limit_bytes=VMEM_LIMIT)


def _group_of_tile(i, tm):
    p_tiles = P_TOK // tm
    return jnp.where(i < p_tiles, 0, 1 + (i - p_tiles) // (DEC_SEQ // tm))


def _mod_spec(layer, k, tm):
    return pl.BlockSpec((None, None, None, 1, D_MODEL), lambda i, *_: (layer, _group_of_tile(i, tm), k, 0, 0))


HALO = 16


def _halo_specs(tm, width, col_block):
    r = tm // HALO
    last = N_TOK // HALO - 1
    prev = pl.BlockSpec((HALO, width), lambda i, *_: (jnp.maximum(i * r - 1, 0), col_block))
    nxt = pl.BlockSpec((HALO, width), lambda i, *_: (jnp.minimum((i + 1) * r, last), col_block))
    return prev, nxt


def _dwconv3(x, prev_row, next_row, w, i):
    tm = x.shape[0]
    assert SEQ % tm == 0 and DEC_SEQ % tm == 0
    first = i * tm
    seqlen = jnp.where(first < P_TOK, SEQ, DEC_SEQ)
    prev_row = jnp.where((first & (seqlen - 1)) == 0, 0.0, prev_row)
    next_row = jnp.where(((first + tm) & (seqlen - 1)) == 0, 0.0, next_row)
    r = lax.broadcasted_iota(jnp.int32, (tm, 1), 0)
    xp = jnp.where(r == 0, prev_row, pltpu.roll(x, 1, 0))
    xn = jnp.where(r == tm - 1, next_row, pltpu.roll(x, tm - 1, 0))
    return xp * w[0:1, :] + x * w[1:2, :] + xn * w[2:3, :]


def _sigmoid(x):
    return jax.nn.sigmoid(x)


def _silu(x):
    return x * jax.nn.sigmoid(x)


def _modulated_norm(x, g, shift, scale):
    y = x * lax.rsqrt(jnp.mean(x * x, axis=-1, keepdims=True) + EPS)
    return (y * g) * (1.0 + scale) + shift


def _mod_kernel(c_ref, w_ref, b_ref, o_ref):
    s = _silu(c_ref[...])
    o_ref[...] = jnp.dot(s.astype(BF16), w_ref[...].astype(BF16), preferred_element_type=F32) + b_ref[...]


def _modulations(cmat, w_ada, b_ada):
    tn = 1024
    out = pl.pallas_call(
        _mod_kernel,
        grid=(DEPTH, N_MOD * D_MODEL // tn),
        in_specs=[pl.BlockSpec((SUBLANE, D_MODEL), lambda l, j: (0, 0)),
                  pl.BlockSpec((None, D_MODEL, tn), lambda l, j: (l, 0, j)),
                  pl.BlockSpec((None, 1, tn), lambda l, j: (l, 0, j))],
        out_specs=pl.BlockSpec((None, SUBLANE, tn), lambda l, j: (l, 0, j)),
        out_shape=jax.ShapeDtypeStruct((DEPTH, SUBLANE, N_MOD * D_MODEL), F32),
        compiler_params=_cparams("arbitrary", "arbitrary"),
        name="ada_modulation",
    )(cmat, w_ada, b_ada.reshape(DEPTH, 1, N_MOD * D_MODEL))
    return out.reshape(DEPTH, SUBLANE, N_MOD, 1, D_MODEL)


PROJ_CHUNK = 1536


PROJ_TM = 512


def _proj_kernel(x_ref, sh_ref, sc_ref, g_ref, w_ref, cos_ref, sin_ref, qg_ref, kg_ref,
                 o_ref, qo_ref, ko_ref, vo_ref, kn_ref, v_ref):
    h = _modulated_norm(x_ref[...], g_ref[...], sh_ref[...], sc_ref[...]).astype(BF16)
    q = kv = None
    for lo in range(0, PROJ_W, PROJ_CHUNK):
        res = jnp.dot(h, w_ref[:, lo:lo + PROJ_CHUNK], preferred_element_type=F32)
        o_ref[:, lo:lo + PROJ_CHUNK] = res.astype(o_ref.dtype)
        if lo <= COL_Q and COL_Q + 4 * LANE <= lo + PROJ_CHUNK:
            q = res[:, COL_Q - lo:COL_Q - lo + 4 * LANE]
        if lo <= COL_KV and COL_KV + 2 * LANE <= lo + PROJ_CHUNK:
            kv = res[:, COL_KV - lo:COL_KV - lo + 2 * LANE]
    _attention_operands(q, kv, cos_ref[...], sin_ref[...], qg_ref[...], kg_ref[...],
                        qo_ref, ko_ref, vo_ref, kn_ref, v_ref)


def _input_projection(x, mods, layer, g, w, cos_t, sin_t, qg, kg):
    tm = PROJ_TM
    w_spec = pl.BlockSpec((D_MODEL, PROJ_W), lambda i: (0, 0), pipeline_mode=pl.Buffered(1))
    tab = lambda i: jnp.where(i < P_TOK // tm, 0, 1 + (i - P_TOK // tm) % (DEC_SEQ // tm))
    row = lambda width: pl.BlockSpec((tm, width), lambda i: (i, 0))
    one = lambda width: pl.BlockSpec((1, width), lambda i: (0, 0))
    wide = jax.ShapeDtypeStruct((N_TOK, 4 * LANE), BF16)
    slab = jax.ShapeDtypeStruct((N_TOK, LANE), F32)
    return pl.pallas_call(
        _proj_kernel,
        grid=(N_TOK // tm,),
        in_specs=[row(D_MODEL), _mod_spec(layer, 0, tm), _mod_spec(layer, 1, tm), one(D_MODEL), w_spec,
                  pl.BlockSpec((tm, LANE), lambda i: (tab(i), 0)), pl.BlockSpec((tm, LANE), lambda i: (tab(i), 0)),
                  one(LANE), one(LANE)],
        out_specs=[row(PROJ_W), row(4 * LANE), row(4 * LANE), row(4 * LANE), row(LANE), row(LANE)],
        out_shape=[jax.ShapeDtypeStruct((N_TOK, PROJ_W), BF16), wide, wide, wide, slab, slab],
        compiler_params=_cparams("arbitrary"),
        name="input_projection",
    )(x, mods, mods, g, w, cos_t, sin_t, qg, kg)


def _pair_norm(x, gain):
    lane = lax.broadcasted_iota(jnp.int32, x.shape, 1)
    x2 = x * x
    lo = jnp.sum(jnp.where(lane < HEAD_DIM, x2, 0.0), axis=-1, keepdims=True)
    hi = jnp.sum(jnp.where(lane >= HEAD_DIM, x2, 0.0), axis=-1, keepdims=True)
    inv = jnp.where(lane < HEAD_DIM, lax.rsqrt(lo / HEAD_DIM + EPS), lax.rsqrt(hi / HEAD_DIM + EPS))
    return x * inv * gain


def _pair_rope(x, cos, sin):
    lane = lax.broadcasted_iota(jnp.int32, x.shape, 1)
    swapped = jnp.where((lane & 1) == 0, pltpu.roll(x, LANE - 1, 1), pltpu.roll(x, 1, 1))
    return x * cos + swapped * sin


def _lane_halves(x):
    lane = lax.broadcasted_iota(jnp.int32, x.shape, 1)
    xr = pltpu.roll(x, HEAD_DIM, 1)
    lo, hi = lane < HEAD_DIM, lane >= HEAD_DIM
    return (jnp.where(lo, x, 0.0), jnp.where(hi, xr, 0.0), jnp.where(lo, xr, 0.0), jnp.where(hi, x, 0.0))


def _attention_operands(q, kv, cos, sin, qg, kg, qo_ref, ko_ref, vo_ref, kn_ref, v_ref):
    for p in range(H_ATT // 2):
        qn = _pair_norm(q[:, p * LANE:(p + 1) * LANE], qg)
        qo_ref[:, p * LANE:(p + 1) * LANE] = (_pair_rope(qn, cos, sin) * (HEAD_DIM ** -0.5)).astype(BF16)
    k = _pair_norm(kv[:, 0:LANE], kg)
    v = kv[:, LANE:2 * LANE]
    kn_ref[...] = k
    v_ref[...] = v
    for j, (kx, vx) in enumerate(zip(_lane_halves(_pair_rope(k, cos, sin)), _lane_halves(v))):
        ko_ref[:, j * LANE:(j + 1) * LANE] = kx.astype(BF16)
        vo_ref[:, j * LANE:(j + 1) * LANE] = vx.astype(BF16)


def _rope_tables(tile):
    pos = jnp.arange(DEC_SEQ)
    row = (pos // GRID_W).astype(F32)
    col = (pos % GRID_W).astype(F32)
    n_freq = HEAD_DIM // 4
    inv_freq = ROPE_BASE ** (-jnp.arange(n_freq, dtype=F32) / n_freq)
    ang = jnp.concatenate([row[:, None] * inv_freq, col[:, None] * inv_freq], axis=-1)
    cos = jnp.repeat(jnp.cos(ang), 2, axis=-1)
    sin = jnp.stack([-jnp.sin(ang), jnp.sin(ang)], axis=-1).reshape(DEC_SEQ, HEAD_DIM)
    cos = jnp.concatenate([jnp.ones((tile, HEAD_DIM), F32), cos], axis=0)
    sin = jnp.concatenate([jnp.zeros((tile, HEAD_DIM), F32), sin], axis=0)
    return jnp.tile(cos, (1, 2)), jnp.tile(sin, (1, 2))


def _attend_kv_head(q2, kx, vx, sink_a, sink_b, mask, kvh):
    nq = q2.shape[0] // 2
    row = lax.broadcasted_iota(jnp.int32, (2 * nq, 1), 0)
    out = None
    for half in range(2):
        j = 2 * kvh + half
        s = lax.dot_general(q2, kx[:, j * LANE:(j + 1) * LANE], (((1,), (1,)), ((), ())), preferred_element_type=F32)
        if mask is not None:
            s = jnp.where(mask, s, NEG)
        sink = jnp.where(row < nq, sink_a[half], sink_b[half])
        m = jnp.maximum(jnp.max(s, axis=-1, keepdims=True), sink)
        e = jnp.exp(s - m)
        denom = jnp.sum(e, axis=-1, keepdims=True) + jnp.exp(sink - m)
        o = jnp.dot(e.astype(BF16), vx[:, j * LANE:(j + 1) * LANE], preferred_element_type=F32) / denom
        out = o if out is None else out + o
    return out


def _ctx_attn_kernel(sink_ref, q_ref, k_ref, v_ref, o_ref):
    kx, vx = k_ref[...], v_ref[...]
    for kvh in range(KV_HEADS):
        p0, p1 = 2 * kvh, 2 * kvh + 1
        q2 = jnp.concatenate([q_ref[:, p0 * LANE:(p0 + 1) * LANE], q_ref[:, p1 * LANE:(p1 + 1) * LANE]], axis=0)
        sa = (sink_ref[2 * p0], sink_ref[2 * p0 + 1])
        sb = (sink_ref[2 * p1], sink_ref[2 * p1 + 1])
        o = _attend_kv_head(q2, kx, vx, sa, sb, None, kvh)
        o_ref[:, p0 * LANE:(p0 + 1) * LANE] = o[:SEQ]
        o_ref[:, p1 * LANE:(p1 + 1) * LANE] = o[SEQ:]


def _context_attention(sink, qs, kx, vx):
    blk = lambda: pl.BlockSpec((SEQ, 4 * LANE), lambda b: (b, 0))
    return pl.pallas_call(
        _ctx_attn_kernel,
        grid=(BATCH,),
        in_specs=[pl.BlockSpec(memory_space=pltpu.SMEM), blk(), blk(), blk()],
        out_specs=blk(),
        out_shape=jax.ShapeDtypeStruct((P_TOK, 4 * LANE), F32),
        compiler_params=_cparams("arbitrary"),
        name="context_attention",
    )(sink, qs, kx, vx)


LAT_BLOCK = 128


def _lat_attn_kernel(sink_ref, q_ref, kp_ref, kc_ref, kn_ref, vp_ref, vc_ref, vn_ref, ck_ref, cv_ref, o_ref):
    n = pl.program_id(1)
    nb = pl.num_programs(1)
    ck = jnp.concatenate(_lane_halves(ck_ref[...]), axis=1).astype(BF16)
    cv = jnp.concatenate(_lane_halves(cv_ref[...]), axis=1).astype(BF16)
    kx = jnp.concatenate([kp_ref[...], kc_ref[...], kn_ref[...], ck], axis=0)
    vx = jnp.concatenate([vp_ref[...], vc_ref[...], vn_ref[...], cv], axis=0)
    n_keys = 3 * LAT_BLOCK + PAST_LEN
    qi = lax.broadcasted_iota(jnp.int32, (2 * LAT_BLOCK, n_keys), 0) & (LAT_BLOCK - 1)
    kj = lax.broadcasted_iota(jnp.int32, (2 * LAT_BLOCK, n_keys), 1)
    far = 2 * LAT_BLOCK
    off_prev = jnp.where(n > 0, 0, far)
    off_next = jnp.where(n < nb - 1, 0, far)
    t_prev = jnp.where(kj < LAT_BLOCK, kj - qi - off_prev, 0)
    t_next = jnp.where(kj >= 2 * LAT_BLOCK, jnp.where(kj < 3 * LAT_BLOCK, qi - off_next - (kj - 2 * LAT_BLOCK), 0), 0)
    mask = jnp.minimum(t_prev, t_next) >= 0
    for kvh in range(KV_HEADS):
        p0, p1 = 2 * kvh, 2 * kvh + 1
        q2 = jnp.concatenate([q_ref[:, p0 * LANE:(p0 + 1) * LANE], q_ref[:, p1 * LANE:(p1 + 1) * LANE]], axis=0)
        sa = (sink_ref[2 * p0], sink_ref[2 * p0 + 1])
        sb = (sink_ref[2 * p1], sink_ref[2 * p1 + 1])
        o = _attend_kv_head(q2, kx, vx, sa, sb, mask, kvh)
        o_ref[:, p0 * LANE:(p0 + 1) * LANE] = o[:LAT_BLOCK]
        o_ref[:, p1 * LANE:(p1 + 1) * LANE] = o[LAT_BLOCK:]


def _latent_attention(sink, qs, kx, vx, cache_k, cache_v, layer):
    nb = DEC_SEQ // LAT_BLOCK
    base = P_TOK // LAT_BLOCK
    cur = lambda b, n: (base + b * nb + n, 0)
    prev = lambda b, n: (base + b * nb + jnp.maximum(n - 1, 0), 0)
    nxt = lambda b, n: (base + b * nb + jnp.minimum(n + 1, nb - 1), 0)
    blk = lambda f: pl.BlockSpec((LAT_BLOCK, 4 * LANE), f)
    ctx = pl.BlockSpec((None, None, PAST_LEN, LANE), lambda b, n: (b, layer, 0, 0))
    return pl.pallas_call(
        _lat_attn_kernel,
        grid=(DEC_BATCH, nb),
        in_specs=[pl.BlockSpec(memory_space=pltpu.SMEM), blk(cur),
                  blk(prev), blk(cur), blk(nxt), blk(prev), blk(cur), blk(nxt), ctx, ctx],
        out_specs=pl.BlockSpec((LAT_BLOCK, 4 * LANE), lambda b, n: (b * nb + n, 0)),
        out_shape=jax.ShapeDtypeStruct((S_TOK, 4 * LANE), F32),
        compiler_params=_cparams("arbitrary", "arbitrary"),
        name="latent_attention",
    )(sink, qs, kx, kx, kx, vx, vx, vx, cache_k, cache_v)


DN_W = DN_HEADS * DN_DK
N_GB = 2 * 2 * DN_HEADS
DN_ROWS = DN_HEADS * DN_CHUNK
DN_TILE = SEQ
CH_PER_TILE = DN_TILE // DN_CHUNK
SEQ_GROUP = DEC_BATCH
P_GC = (BATCH // SEQ_GROUP) * (SEQ // DN_CHUNK)
N_GC = P_GC + DEC_SEQ // DN_CHUNK


def _dot_hi(a, b):
    return jnp.dot(a, b, precision=lax.Precision.HIGHEST, preferred_element_type=F32)


def _dot_nt(a, b):
    return lax.dot_general(a.astype(BF16), b.astype(BF16), (((1,), (1,)), ((), ())), preferred_element_type=F32)


def _dot_tn(a, b):
    return lax.dot_general(a.astype(BF16), b.astype(BF16), (((0,), (0,)), ((), ())), preferred_element_type=F32)


def _dot_bf(a, b):
    return jnp.dot(a.astype(BF16), b.astype(BF16), preferred_element_type=F32)


def _stack_heads(x):
    return jnp.concatenate([x[:, h * DN_DK:(h + 1) * DN_DK] for h in range(DN_HEADS)], axis=0)


def _stack_cols(m, c0):
    return jnp.concatenate([m[:, c0 + h:c0 + h + 1] for h in range(DN_HEADS)], axis=0)


def _head(x, h):
    return x[h * DN_CHUNK:(h + 1) * DN_CHUNK]


def _per_head(f):
    return jnp.concatenate([f(h) for h in range(DN_HEADS)], axis=0)


def _tri_inverses(lms, xor):
    eye = (xor == 0).astype(F32)
    ts = [eye - jnp.where(xor < 2, lm, 0.0) for lm in lms]
    k = 1
    while (2 << k) <= DN_CHUNK:
        level = (xor >> k) == 1
        ms = [jnp.where(level, lm, 0.0).astype(BF16) for lm in lms]
        tbs = [t.astype(BF16) for t in ts]
        mts = [_per_head(lambda h: _dot_bf(_head(m, h), _head(tb, h))).astype(BF16) for m, tb in zip(ms, tbs)]
        ts = [t - _per_head(lambda h: _dot_bf(_head(tb, h), _head(mt, h))) for t, tb, mt in zip(ts, tbs, mts)]
        k += 1
    return ts


def _dn_tile_slot(i):
    p_tiles = P_TOK // DN_TILE
    tiles_per_seq = DEC_SEQ // DN_TILE
    j = i - p_tiles
    blk = jnp.where(i < p_tiles, i // SEQ_GROUP, P_GC // CH_PER_TILE + j % tiles_per_seq)
    q = jnp.where(i < p_tiles, i % SEQ_GROUP, j // tiles_per_seq)
    return blk, q


def _dn_local_kernel(x_ref, xp_ref, xn_ref, ab_ref, w_ref, alog_ref, dtb_ref,
                     u_ref, w_out_ref, qd_ref, kd_ref, qk_ref, egl_ref):
    s = _silu(_dwconv3(x_ref[...].astype(F32), xp_ref[...].astype(F32)[HALO - 1:HALO, :],
                       xn_ref[...].astype(F32)[0:1, :], w_ref[...], pl.program_id(0)))
    ab = ab_ref[...].astype(F32)
    lane = lax.broadcasted_iota(jnp.int32, ab.shape, 1)
    z = ab + dtb_ref[...]
    softplus = jnp.maximum(z, 0.0) + jnp.log1p(jnp.exp(-jnp.abs(z)))
    gb = jnp.where(lane < N_GB // 2, -jnp.exp(alog_ref[...]) * softplus, jax.nn.sigmoid(ab))

    c, r = DN_CHUNK, DN_ROWS
    ri = lax.broadcasted_iota(jnp.int32, (r, c), 0) & (c - 1)
    ci = lax.broadcasted_iota(jnp.int32, (r, c), 1)
    xor = ri ^ ci
    masks = ((ri >= ci, ri > ci), (ri <= ci, ri < ci))
    tri = (lax.broadcasted_iota(jnp.int32, (c, c), 0) >= lax.broadcasted_iota(jnp.int32, (c, c), 1)).astype(F32)

    problems, lms = [], []
    for ch in range(CH_PER_TILE):
        sc = s[ch * c:(ch + 1) * c]
        qst = _stack_heads(sc[:, 0:DN_W])
        kst = _stack_heads(sc[:, DN_W:2 * DN_W])
        vst = _stack_heads(sc[:, 2 * DN_W:3 * DN_W])
        qst = qst * lax.rsqrt(jnp.sum(qst * qst, axis=-1, keepdims=True) + EPS) * (DN_DK ** -0.5)
        kst = kst * lax.rsqrt(jnp.sum(kst * kst, axis=-1, keepdims=True) + EPS)
        gbc = gb[ch * c:(ch + 1) * c]
        prefix = _dot_hi(tri, gbc)
        suffix = prefix[c - 1:c] - prefix + gbc
        qb, kb = qst.astype(BF16), kst.astype(BF16)
        gram = _per_head(lambda h: _dot_nt(_head(kb, h), _head(kb, h)))
        qk0 = _per_head(lambda h: _dot_nt(_head(qb, h), _head(kb, h)))
        for d, (gc, last) in enumerate(((prefix, c - 1), (suffix, 0))):
            incl, strict = masks[d]
            c0 = d * DN_HEADS
            gcol = _stack_cols(gc, c0)
            beta = _stack_cols(gbc, N_GB // 2 + c0)
            gct = gc.T
            grow = _per_head(lambda h: jnp.broadcast_to(gct[c0 + h:c0 + h + 1, :], (c, c)))
            decay = jnp.exp(jnp.where(incl, gcol - grow, NEG))
            lms.append(jnp.where(strict, gram * beta * decay, 0.0))
            eg = jnp.exp(gcol)
            glast = _per_head(lambda h: jnp.broadcast_to(gc[last:last + 1, c0 + h:c0 + h + 1], (c, 1)))
            qd_ref[ch, d] = (qst * eg).astype(BF16)
            kd_ref[ch, d] = (kst * jnp.exp(glast - gcol)).astype(BF16)
            qk_ref[ch, d] = jnp.where(incl, qk0 * decay, 0.0).astype(BF16)
            egl_ref[ch, d] = jnp.concatenate([jnp.broadcast_to(jnp.exp(gc[last:last + 1, c0 + h:c0 + h + 1]), (1, DN_DV))
                                              for h in range(DN_HEADS)], axis=1)
            problems.append((ch, d, jnp.concatenate([vst * beta, kst * (beta * eg)], axis=1).astype(BF16)))

    for (ch, d, rhs), t in zip(problems, _tri_inverses(lms, xor)):
        tb = t.astype(BF16)
        x = _per_head(lambda h: jnp.dot(_head(tb, h), _head(rhs, h), preferred_element_type=F32))
        u_ref[ch, d] = x[:, 0:DN_DV]
        w_out_ref[ch, d] = x[:, DN_DV:2 * DN_DV].astype(BF16)


def _dn_local(proj, conv_w, alog, dtb):
    wq = 3 * DN_W
    prev, nxt = _halo_specs(DN_TILE, wq, COL_DQKV // wq)

    def slot_spec(*tail):
        return pl.BlockSpec((CH_PER_TILE, None, 2) + tail,
                            lambda i: _dn_tile_slot(i) + (0,) * (1 + len(tail)))

    def slot_shape(dtype, *tail):
        return jax.ShapeDtypeStruct((N_GC, SEQ_GROUP, 2) + tail, dtype)

    return pl.pallas_call(
        _dn_local_kernel,
        grid=(N_TOK // DN_TILE,),
        in_specs=[pl.BlockSpec((DN_TILE, wq), lambda i: (i, COL_DQKV // wq)), prev, nxt,
                  pl.BlockSpec((DN_TILE, LANE), lambda i: (i, COL_AB // LANE)),
                  pl.BlockSpec((3, wq), lambda i: (0, 0)),
                  pl.BlockSpec((1, LANE), lambda i: (0, 0)),
                  pl.BlockSpec((1, LANE), lambda i: (0, 0))],
        out_specs=[slot_spec(DN_ROWS, DN_DV), slot_spec(DN_ROWS, DN_DK), slot_spec(DN_ROWS, DN_DK),
                   slot_spec(DN_ROWS, DN_DK), slot_spec(DN_ROWS, DN_CHUNK), slot_spec(1, DN_W)],
        out_shape=[slot_shape(F32, DN_ROWS, DN_DV), slot_shape(BF16, DN_ROWS, DN_DK), slot_shape(BF16, DN_ROWS, DN_DK),
                   slot_shape(BF16, DN_ROWS, DN_DK), slot_shape(BF16, DN_ROWS, DN_CHUNK), slot_shape(F32, 1, DN_W)],
        compiler_params=_cparams("arbitrary"),
        name="deltanet_local",
    )(proj, proj, proj, proj, conv_w, alog, dtb)


def _scan_step(t):
    pc, sc = SEQ // DN_CHUNK, DEC_SEQ // DN_CHUNK
    is_p = t < P_GC
    g = jnp.where(is_p, t // pc, BATCH // SEQ_GROUP)
    s = jnp.where(is_p, t % pc, t - P_GC)
    n = jnp.where(is_p, pc, sc)
    return g, s, n, t, t - s + (n - 1 - s)


def _dn_scan_kernel(uf_ref, wf_ref, qdf_ref, kdf_ref, qkf_ref, egf_ref, ub_ref, wb_ref, qdb_ref, kdb_ref, qkb_ref,
                    egb_ref, s0f_ref, s0b_ref, of_ref, ob_ref, sf_ref, sb_ref, state_ref):
    _, s, n, _, _ = _scan_step(pl.program_id(0))
    c, r = DN_CHUNK, DN_ROWS
    heads = range(DN_HEADS)

    @pl.when(s == 0)
    def _():
        for q in range(SEQ_GROUP):
            state_ref[0, q] = jnp.concatenate([s0f_ref[q, h] for h in heads], axis=1)
            state_ref[1, q] = jnp.concatenate([s0b_ref[q, h] for h in heads], axis=1)

    dirs = ((uf_ref, wf_ref, qdf_ref, kdf_ref, qkf_ref, egf_ref, of_ref),
            (ub_ref, wb_ref, qdb_ref, kdb_ref, qkb_ref, egb_ref, ob_ref))
    chains = [(d, q) + refs for d, refs in enumerate(dirs) for q in range(SEQ_GROUP)]
    states = [state_ref[d, q] for d, q, *_ in chains]
    def state_products(w, qd, st):
        sb = st.astype(BF16)
        both = [jnp.dot(jnp.concatenate([_head(w, h), _head(qd, h)], axis=0), sb[:, h * DN_DV:(h + 1) * DN_DV],
                        preferred_element_type=F32) for h in heads]
        return (jnp.concatenate([b[0:c] for b in both], axis=0), jnp.concatenate([b[c:2 * c] for b in both], axis=0))

    boths = [state_products(w_ref[q], qd_ref[q], st) for (d, q, u_ref, w_ref, qd_ref, *_), st in zip(chains, states)]
    v_news = [(u_ref[q] - ws).astype(BF16) for (d, q, u_ref, *_), (ws, _) in zip(chains, boths)]
    for (d, q, u_ref, w_ref, qd_ref, kd_ref, qk_ref, eg_ref, o_ref), st, (_, qs), v_new in zip(chains, states, boths, v_news):
        qk = qk_ref[q]
        o = qs + _per_head(lambda h: jnp.dot(_head(qk, h), _head(v_new, h), preferred_element_type=F32))
        kd = kd_ref[q]
        outer = jnp.concatenate([lax.dot_general(_head(kd, h), _head(v_new, h), (((0,), (0,)), ((), ())),
                                                 preferred_element_type=F32) for h in heads], axis=1)
        state_ref[d, q] = st * eg_ref[q] + outer
        o_ref[q] = jnp.concatenate([o[h * c:(h + 1) * c] for h in heads], axis=1)

    @pl.when(s == n - 1)
    def _():
        for q in range(SEQ_GROUP):
            for h in heads:
                sf_ref[q, h] = state_ref[0, q][:, h * DN_DV:(h + 1) * DN_DV]
                sb_ref[q, h] = state_ref[1, q][:, h * DN_DV:(h + 1) * DN_DV]


def _dn_scan(u, w, qd, kd, qk, egl, s0f, s0b):
    fwd = lambda t: _scan_step(t)[3]
    bwd = lambda t: _scan_step(t)[4]
    grp = lambda t: _scan_step(t)[0]

    def specs(slot, d):
        sp = lambda *tail: pl.BlockSpec((None, SEQ_GROUP, None) + tail, lambda t: (slot(t), 0, d) + (0,) * len(tail))
        return [sp(DN_ROWS, DN_DV), sp(DN_ROWS, DN_DK), sp(DN_ROWS, DN_DK), sp(DN_ROWS, DN_DK), sp(DN_ROWS, DN_CHUNK),
                sp(1, DN_W)]

    st = lambda: pl.BlockSpec((SEQ_GROUP, DN_HEADS, DN_DK, DN_DV), lambda t: (grp(t), 0, 0, 0))
    o_spec = lambda slot: pl.BlockSpec((None, SEQ_GROUP, DN_CHUNK, DN_W), lambda t: (slot(t), 0, 0, 0))
    o_shape = jax.ShapeDtypeStruct((N_GC, SEQ_GROUP, DN_CHUNK, DN_W), F32)
    st_shape = jax.ShapeDtypeStruct((N_SEQ, DN_HEADS, DN_DK, DN_DV), F32)
    args = (u, w, qd, kd, qk, egl)
    return pl.pallas_call(
        _dn_scan_kernel,
        grid=(N_GC,),
        in_specs=specs(fwd, 0) + specs(bwd, 1) + [st(), st()],
        out_specs=[o_spec(fwd), o_spec(bwd), st(), st()],
        out_shape=[o_shape, o_shape, st_shape, st_shape],
        scratch_shapes=[pltpu.VMEM((2, SEQ_GROUP, DN_DK, DN_W), F32)],
        compiler_params=_cparams("arbitrary"),
        name="deltanet_scan",
    )(*args, *args, s0f, s0b)


def _merge_kernel(a_ref, ap_ref, an_ref, yb_ref, of_ref, ob_ref, z_ref, ga_ref, gb_ref, gc_ref, x_ref, g1_ref,
                  cw_ref, ng_ref, wa_ref, wb_ref, wc_ref, wo_ref, o_ref):
    tm = x_ref.shape[0]
    w = A_WIDTH
    a = a_ref[...].astype(F32)
    ap = ap_ref[...].astype(F32)[HALO - 1:HALO, :]
    an = an_ref[...].astype(F32)[0:1, :]
    u = a[:, w:2 * w] * a[:, 2 * w:3 * w]
    up = ap[:, w:2 * w] * ap[:, 2 * w:3 * w]
    un = an[:, w:2 * w] * an[:, 2 * w:3 * w]
    y_a = a[:, 0:w] * _dwconv3(u, up, un, cw_ref[...], pl.program_id(0))
    merged = _sigmoid(ga_ref[...].astype(F32)) * _dot_bf(y_a, wa_ref[...])
    merged += _sigmoid(gb_ref[...].astype(F32)) * _dot_bf(yb_ref[...], wb_ref[...])
    o_sum = (of_ref[...] + ob_ref[...]).reshape(tm, DN_W)
    parts = []
    for h in range(DN_HEADS):
        sl = slice(h * DN_DV, (h + 1) * DN_DV)
        o = o_sum[:, sl]
        o = o * lax.rsqrt(jnp.mean(o * o, axis=-1, keepdims=True) + EPS) * ng_ref[...]
        parts.append(o * _silu(z_ref[:, sl].astype(F32)))
    y_c = jnp.concatenate(parts, axis=1)
    merged += _sigmoid(gc_ref[...].astype(F32)) * _dot_bf(y_c, wc_ref[...])
    o_ref[...] = x_ref[...] + g1_ref[...] * _dot_bf(merged, wo_ref[...])


def _merge(proj, y_b, o_f, o_b, x, mods, layer, conv_a_w, dn_norm_g, wa, wb, wc, wo):
    tm = DN_TILE
    wa3 = 3 * A_WIDTH
    prev, nxt = _halo_specs(tm, wa3, COL_A // wa3)
    tok = lambda w, cb=0: pl.BlockSpec((tm, w), lambda i: (i, cb))
    full = lambda a: pl.BlockSpec(a.shape, lambda i: (0,) * a.ndim)
    scan_out = lambda: pl.BlockSpec((CH_PER_TILE, None, DN_CHUNK, DN_W), lambda i: _dn_tile_slot(i) + (0, 0))
    gate0 = COL_GATES // D_MODEL
    return pl.pallas_call(
        _merge_kernel,
        grid=(N_TOK // tm,),
        in_specs=[tok(wa3, COL_A // wa3), prev, nxt, tok(A_WIDTH), scan_out(), scan_out(), tok(DN_W, COL_DZ // DN_W),
                  tok(D_MODEL, gate0), tok(D_MODEL, gate0 + 1), tok(D_MODEL, gate0 + 2), tok(D_MODEL),
                  _mod_spec(layer, 2, tm), full(conv_a_w), full(dn_norm_g), full(wa), full(wb), full(wc), full(wo)],
        out_specs=tok(D_MODEL),
        out_shape=jax.ShapeDtypeStruct((N_TOK, D_MODEL), F32),
        compiler_params=_cparams("arbitrary"),
        name="mixer_merge",
    )(proj, proj, proj, y_b, o_f, o_b, proj, proj, proj, proj, x, mods, conv_a_w, dn_norm_g, wa, wb, wc, wo)


FF_CHUNK = 256


def _swiglu_chunked(h, wg_ref, wu_ref, wd_ref):
    width = wg_ref.shape[1]
    bounds = list(range(0, width, FF_CHUNK)) + [width]
    if bounds[-1] - bounds[-2] < FF_CHUNK and len(bounds) > 2:
        del bounds[-2]
    chunks = list(zip(bounds[:-1], bounds[1:]))
    gate_up = lambda lo, hi: (jnp.dot(h, wg_ref[:, lo:hi], preferred_element_type=F32),
                              jnp.dot(h, wu_ref[:, lo:hi], preferred_element_type=F32))
    out = None
    nxt = gate_up(*chunks[0])
    for c, (lo, hi) in enumerate(chunks):
        g, u = nxt
        if c + 1 < len(chunks):
            nxt = gate_up(*chunks[c + 1])
        y = jnp.dot((_silu(g) * u).astype(BF16), wd_ref[lo:hi, :], preferred_element_type=F32)
        out = y if out is None else out + y
    return out


def _ffn_kernel(x_ref, sh_ref, sc_ref, g2_ref, g_ref, wg_ref, wu_ref, wd_ref, o_ref, h_ref, acc_ref):
    f = pl.program_id(1)

    @pl.when(f == 0)
    def _():
        h_ref[...] = _modulated_norm(x_ref[...], g_ref[...], sh_ref[...], sc_ref[...]).astype(BF16)
        acc_ref[...] = jnp.zeros_like(acc_ref)

    acc_ref[...] += _swiglu_chunked(h_ref[...], wg_ref, wu_ref, wd_ref)

    @pl.when(f == pl.num_programs(1) - 1)
    def _():
        o_ref[...] = x_ref[...] + g2_ref[...] * acc_ref[...]


def _dense_ffn(x, mods, layer, g, wg, wu, wd):
    tm, tf = 1024, 1408
    return pl.pallas_call(
        _ffn_kernel,
        grid=(N_TOK // tm, FF_DENSE // tf),
        in_specs=[pl.BlockSpec((tm, D_MODEL), lambda i, f: (i, 0)),
                  _mod_spec(layer, 3, tm), _mod_spec(layer, 4, tm), _mod_spec(layer, 5, tm),
                  pl.BlockSpec((1, D_MODEL), lambda i, f: (0, 0)),
                  pl.BlockSpec((D_MODEL, tf), lambda i, f: (0, f)),
                  pl.BlockSpec((D_MODEL, tf), lambda i, f: (0, f)),
                  pl.BlockSpec((tf, D_MODEL), lambda i, f: (f, 0))],
        out_specs=pl.BlockSpec((tm, D_MODEL), lambda i, f: (i, 0)),
        out_shape=jax.ShapeDtypeStruct((N_TOK, D_MODEL), F32),
        scratch_shapes=[pltpu.VMEM((tm, D_MODEL), BF16), pltpu.VMEM((tm, D_MODEL), F32)],
        compiler_params=_cparams("arbitrary", "arbitrary"),
        name="dense_ffn",
    )(x, mods, mods, mods, g, wg, wu, wd)


MOE_TM = 1024
MOE_BLOCKS = ((0, 256), (256, 128), (384, 128), (512, 256), (768, 256))
ROUTE_ROWS = 16


def _moe_kernel(x_ref, sh_ref, sc_ref, g2_ref, g_ref, wr_ref, br_ref, before_ref, wg_ref, wu_ref, wd_ref, o_ref,
                h_ref, route_ref, acc_ref):
    e = pl.program_id(1)
    sub = lax.broadcasted_iota(jnp.int32, (ROUTE_ROWS, MOE_TM), 0)

    @pl.when(e == 0)
    def _():
        hb = _modulated_norm(x_ref[...], g_ref[...], sh_ref[...], sc_ref[...]).astype(BF16)
        h_ref[...] = hb
        logits = lax.dot_general(wr_ref[...].astype(BF16), hb, (((1,), (1,)), ((), ())),
                                 preferred_element_type=F32) + br_ref[:, 0:1]
        m1 = jnp.max(logits, axis=0, keepdims=True)
        i1 = jnp.min(jnp.where(logits == m1, sub, ROUTE_ROWS), axis=0, keepdims=True)
        rest = jnp.where(sub == i1, -jnp.inf, logits)
        m2 = jnp.max(rest, axis=0, keepdims=True)
        i2 = jnp.min(jnp.where(rest == m2, sub, ROUTE_ROWS), axis=0, keepdims=True)
        e2 = jnp.exp(m2 - m1)
        route_ref[0] = jnp.where(sub == i1, 1.0 / (1.0 + e2), 0.0) + jnp.where(sub == i2, e2 / (1.0 + e2), 0.0)
        sel = jnp.where(sub == i1, 1.0, jnp.where(sub == i2, 1.0, 0.0))
        rank = jnp.dot(sel.astype(BF16), before_ref[...], preferred_element_type=F32)
        route_ref[1] = jnp.where(sel > 0.5, rank, -1.0)
        acc_ref[...] = jnp.zeros_like(acc_ref)

    pick = sub == e
    comb_e = jnp.sum(jnp.where(pick, route_ref[0], 0.0), axis=0, keepdims=True)
    key_e = jnp.sum(jnp.where(pick, route_ref[1], 0.0), axis=0, keepdims=True)
    count = jnp.sum(jnp.where(key_e >= 0.0, 1.0, 0.0))

    for start, rows in MOE_BLOCKS:
        @pl.when(count > start)
        def _():
            r = (lax.broadcasted_iota(jnp.int32, (rows, MOE_TM), 0) + start).astype(F32)
            hit = key_e == r
            onehot = jnp.where(hit, 1.0, 0.0).astype(BF16)
            hs = jnp.dot(onehot, h_ref[...], preferred_element_type=F32).astype(BF16)
            y = _swiglu_chunked(hs, wg_ref, wu_ref, wd_ref)
            w_row = jnp.sum(jnp.where(hit, comb_e, 0.0), axis=1, keepdims=True)
            acc_ref[...] += lax.dot_general(onehot, (y * w_row).astype(BF16), (((0,), (0,)), ((), ())),
                                            preferred_element_type=F32)

    @pl.when(e == pl.num_programs(1) - 1)
    def _():
        o_ref[...] = x_ref[...] + g2_ref[...] * acc_ref[...]


def _moe_ffn(x, mods, layer, g, wr, br, wg, wu, wd):
    tm = MOE_TM
    ew = lambda: pl.BlockSpec((None, D_MODEL, FF_EXPERT), lambda i, e: (e, 0, 0))
    before = jnp.triu(jnp.ones((tm, tm), BF16), k=1)
    return pl.pallas_call(
        _moe_kernel,
        grid=(N_TOK // tm, N_EXPERTS),
        in_specs=[pl.BlockSpec((tm, D_MODEL), lambda i, e: (i, 0)),
                  _mod_spec(layer, 3, tm), _mod_spec(layer, 4, tm), _mod_spec(layer, 5, tm),
                  pl.BlockSpec((1, D_MODEL), lambda i, e: (0, 0)),
                  pl.BlockSpec((ROUTE_ROWS, D_MODEL), lambda i, e: (0, 0)),
                  pl.BlockSpec((ROUTE_ROWS, LANE), lambda i, e: (0, 0)),
                  pl.BlockSpec((tm, tm), lambda i, e: (0, 0)),
                  ew(), ew(), ew()],
        out_specs=pl.BlockSpec((tm, D_MODEL), lambda i, e: (i, 0)),
        out_shape=jax.ShapeDtypeStruct((N_TOK, D_MODEL), F32),
        scratch_shapes=[pltpu.VMEM((tm, D_MODEL), BF16), pltpu.VMEM((2, ROUTE_ROWS, tm), F32),
                        pltpu.VMEM((tm, D_MODEL), F32)],
        compiler_params=_cparams("arbitrary", "arbitrary"),
        name="moe_ffn",
    )(x, mods, mods, mods, g, wr, br, before, wg, wu, wd)


def _reorder_w_in(w):
    pad = jnp.zeros((w.shape[0], PROJ_W - (COL_AB + N_GB)), w.dtype)
    return jnp.concatenate([w[:, 0:1536], w[:, 2304:3840], w[:, 1536:2048], w[:, 3840:4352], w[:, 4368:7440],
                            w[:, 2048:2304], w[:, 4352:4368], pad], axis=1).astype(BF16)


def _pad_lanes(v, fill=0.0):
    v = v.reshape(1, -1)
    return jnp.concatenate([v, jnp.full((1, LANE - v.shape[1]), fill, v.dtype)], axis=1)


def kernel(x_prompt, x_sample, cache_k, cache_v, state_fwd, state_bwd, c, c_ctx, norm1_g, norm2_g, w_ada, b_ada, w_in, conv_a_w, q_norm_g, k_norm_g, attn_sink, dn_conv_w, dn_a_log, dn_dt_bias, dn_norm_g, w_out_a, w_out_b, w_out_c, w_o, w_ff_gate, w_ff_up, w_ff_down, w_router, b_router, w_moe_gate, w_moe_up, w_moe_down):
    x = jnp.concatenate([x_prompt.reshape(P_TOK, D_MODEL), x_sample.reshape(S_TOK, D_MODEL)], axis=0)
    cmat = jnp.concatenate([c_ctx[None, :], c, jnp.zeros((SUBLANE - 1 - DEC_BATCH, D_MODEL), F32)], axis=0)
    mods = _modulations(cmat, w_ada, b_ada)
    cos_t, sin_t = _rope_tables(PROJ_TM)
    ck = cache_k.reshape(DEC_BATCH, DEPTH, PAST_LEN, LANE)
    cv = cache_v.reshape(DEC_BATCH, DEPTH, PAST_LEN, LANE)
    zero_state = jnp.zeros((BATCH, DN_HEADS, DN_DK, DN_DV), F32)

    ks_out, vs_out, sf_out, sb_out = [], [], [], []
    for l in range(DEPTH):
        proj, qs, kx, vx, k_norm, v_raw = _input_projection(
            x, mods, l, norm1_g[l][None, :], _reorder_w_in(w_in[l]), cos_t, sin_t,
            jnp.tile(q_norm_g[l], 2)[None, :], jnp.tile(k_norm_g[l], 2)[None, :])
        y_b = jnp.concatenate([_context_attention(attn_sink[l], qs, kx, vx),
                               _latent_attention(attn_sink[l], qs, kx, vx, ck, cv, l)], axis=0)
        ks_out.append(k_norm[:P_TOK].reshape(BATCH, SEQ, KV_HEADS, HEAD_DIM))
        vs_out.append(v_raw[:P_TOK].reshape(BATCH, SEQ, KV_HEADS, HEAD_DIM))

        local = _dn_local(proj, dn_conv_w[l], _pad_lanes(dn_a_log[l]), _pad_lanes(dn_dt_bias[l]))
        s0f = jnp.concatenate([zero_state, state_fwd[:, l]], axis=0)
        s0b = jnp.concatenate([zero_state, state_bwd[:, l]], axis=0)
        o_f, o_b, s_f, s_b = _dn_scan(*local, s0f, s0b)
        sf_out.append(s_f[:BATCH])
        sb_out.append(s_b[:BATCH])

        x = _merge(proj, y_b, o_f, o_b, x, mods, l, conv_a_w[l], dn_norm_g[l][None, :],
                   w_out_a[l].astype(BF16), w_out_b[l].astype(BF16), w_out_c[l].astype(BF16), w_o[l].astype(BF16))

        i = l // 2
        if l % 2 == 0:
            x = _dense_ffn(x, mods, l, norm2_g[l][None, :], w_ff_gate[i].astype(BF16), w_ff_up[i].astype(BF16),
                           w_ff_down[i].astype(BF16))
        else:
            wr = jnp.concatenate([w_router[i].T, jnp.zeros((ROUTE_ROWS - N_EXPERTS, D_MODEL), F32)], axis=0)
            br = jnp.concatenate([b_router[i], jnp.full((ROUTE_ROWS - N_EXPERTS,), -jnp.inf, F32)])
            x = _moe_ffn(x, mods, l, norm2_g[l][None, :], wr, jnp.broadcast_to(br[:, None], (ROUTE_ROWS, LANE)),
                         w_moe_gate[i].astype(BF16), w_moe_up[i].astype(BF16), w_moe_down[i].astype(BF16))

    return (x[:P_TOK].reshape(BATCH, SEQ, D_MODEL), x[P_TOK:].reshape(DEC_BATCH, DEC_SEQ, D_MODEL),
            jnp.stack(ks_out, axis=1), jnp.stack(vs_out, axis=1), jnp.stack(sf_out, axis=1), jnp.stack(sb_out, axis=1))
```

```python
import functools

import jax
import jax.numpy as jnp
from jax import lax
from jax.experimental import pallas as pl
from jax.experimental.pallas import tpu as pltpu

F32 = jnp.float32
BF16 = jnp.bfloat16

D_MODEL = 1024
BATCH = 32
SEQ = 256
DEPTH = 4
DEC_BATCH = 4
DEC_SEQ = 4096
PAST_LEN = 256
GRID_W = 64
A_WIDTH = 512
HEAD_DIM = 64
H_ATT = 8
KV_HEADS = 2
ROPE_BASE = 10000.0
DN_HEADS = 4
DN_DK = 128
DN_DV = 128
DN_CHUNK = 64
FF_DENSE = 2816
N_EXPERTS = 8
FF_EXPERT = 1024
N_MOD = 6
EPS = 1e-6
NEG = -1e30

P_TOK = BATCH * SEQ
S_TOK = DEC_BATCH * DEC_SEQ
N_TOK = P_TOK + S_TOK
N_SEQ = BATCH + DEC_BATCH
LANE = 128
SUBLANE = 8

COL_A = 0
COL_DQKV = 1536
COL_Q = 3072
COL_DZ = 3584
COL_GATES = 4096
COL_KV = 7168
COL_AB = 7424
PROJ_W = 7680

VMEM_LIMIT = 56 * 1024 * 1024


def _cparams(*sem):
    return pltpu.CompilerParams(dimension_semantics=sem, vmem_limit_bytes=VMEM_LIMIT)


def _group_of_tile(i, tm):
    p_tiles = P_TOK // tm
    return jnp.where(i < p_tiles, 0, 1 + (i - p_tiles) // (DEC_SEQ // tm))


def _mod_spec(layer, k, tm):
    return pl.BlockSpec((None, None, None, 1, D_MODEL), lambda i, *_: (layer, _group_of_tile(i, tm), k, 0, 0))


HALO = 16


def _halo_specs(tm, width, col_block):
    r = tm // HALO
    last = N_TOK // HALO - 1
    prev = pl.BlockSpec((HALO, width), lambda i, *_: (jnp.maximum(i * r - 1, 0), col_block))
    nxt = pl.BlockSpec((HALO, width), lambda i, *_: (jnp.minimum((i + 1) * r, last), col_block))
    return prev, nxt


def _dwconv3(x, prev_row, next_row, w, i):
    tm = x.shape[0]
    assert SEQ % tm == 0 and DEC_SEQ % tm == 0
    first = i * tm
    seqlen = jnp.where(first < P_TOK, SEQ, DEC_SEQ)
    prev_row = jnp.where((first & (seqlen - 1)) == 0, 0.0, prev_row)
    next_row = jnp.where(((first + tm) & (seqlen - 1)) == 0, 0.0, next_row)
    r = lax.broadcasted_iota(jnp.int32, (tm, 1), 0)
    xp = jnp.where(r == 0, prev_row, pltpu.roll(x, 1, 0))
    xn = jnp.where(r == tm - 1, next_row, pltpu.roll(x, tm - 1, 0))
    return xp * w[0:1, :] + x * w[1:2, :] + xn * w[2:3, :]


def _sigmoid(x):
    return jax.nn.sigmoid(x)


def _silu(x):
    return x * jax.nn.sigmoid(x)


def _modulated_norm(x, g, shift, scale):
    y = x * lax.rsqrt(jnp.mean(x * x, axis=-1, keepdims=True) + EPS)
    return (y * g) * (1.0 + scale) + shift


def _mod_kernel(c_ref, w_ref, b_ref, o_ref):
    s = _silu(c_ref[...])
    o_ref[...] = jnp.dot(s.astype(BF16), w_ref[...].astype(BF16), preferred_element_type=F32) + b_ref[...]


def _modulations(cmat, w_ada, b_ada):
    tn = 1024
    out = pl.pallas_call(
        _mod_kernel,
        grid=(DEPTH, N_MOD * D_MODEL // tn),
        in_specs=[pl.BlockSpec((SUBLANE, D_MODEL), lambda l, j: (0, 0)),
                  pl.BlockSpec((None, D_MODEL, tn), lambda l, j: (l, 0, j)),
                  pl.BlockSpec((None, 1, tn), lambda l, j: (l, 0, j))],
        out_specs=pl.BlockSpec((None, SUBLANE, tn), lambda l, j: (l, 0, j)),
        out_shape=jax.ShapeDtypeStruct((DEPTH, SUBLANE, N_MOD * D_MODEL), F32),
        compiler_params=_cparams("arbitrary", "arbitrary"),
        name="ada_modulation",
    )(cmat, w_ada, b_ada.reshape(DEPTH, 1, N_MOD * D_MODEL))
    return out.reshape(DEPTH, SUBLANE, N_MOD, 1, D_MODEL)


PROJ_CHUNK = 1536


PROJ_TM = 512


def _proj_kernel(x_ref, sh_ref, sc_ref, g_ref, w_ref, cos_ref, sin_ref, qg_ref, kg_ref,
                 o_ref, qo_ref, ko_ref, vo_ref, kn_ref, v_ref):
    h = _modulated_norm(x_ref[...], g_ref[...], sh_ref[...], sc_ref[...]).astype(BF16)
    q = kv = None
    for lo in range(0, PROJ_W, PROJ_CHUNK):
        res = jnp.dot(h, w_ref[:, lo:lo + PROJ_CHUNK], preferred_element_type=F32)
        o_ref[:, lo:lo + PROJ_CHUNK] = res.astype(o_ref.dtype)
        if lo <= COL_Q and COL_Q + 4 * LANE <= lo + PROJ_CHUNK:
            q = res[:, COL_Q - lo:COL_Q - lo + 4 * LANE]
        if lo <= COL_KV and COL_KV + 2 * LANE <= lo + PROJ_CHUNK:
            kv = res[:, COL_KV - lo:COL_KV - lo + 2 * LANE]
    _attention_operands(q, kv, cos_ref[...], sin_ref[...], qg_ref[...], kg_ref[...],
                        qo_ref, ko_ref, vo_ref, kn_ref, v_ref)


def _input_projection(x, mods, layer, g, w, cos_t, sin_t, qg, kg):
    tm = PROJ_TM
    w_spec = pl.BlockSpec((D_MODEL, PROJ_W), lambda i: (0, 0), pipeline_mode=pl.Buffered(1))
    tab = lambda i: jnp.where(i < P_TOK // tm, 0, 1 + (i - P_TOK // tm) % (DEC_SEQ // tm))
    row = lambda width: pl.BlockSpec((tm, width), lambda i: (i, 0))
    one = lambda width: pl.BlockSpec((1, width), lambda i: (0, 0))
    wide = jax.ShapeDtypeStruct((N_TOK, 4 * LANE), BF16)
    slab = jax.ShapeDtypeStruct((N_TOK, LANE), F32)
    return pl.pallas_call(
        _proj_kernel,
        grid=(N_TOK // tm,),
        in_specs=[row(D_MODEL), _mod_spec(layer, 0, tm), _mod_spec(layer, 1, tm), one(D_MODEL), w_spec,
                  pl.BlockSpec((tm, LANE), lambda i: (tab(i), 0)), pl.BlockSpec((tm, LANE), lambda i: (tab(i), 0)),
                  one(LANE), one(LANE)],
        out_specs=[row(PROJ_W), row(4 * LANE), row(4 * LANE), row(4 * LANE), row(LANE), row(LANE)],
        out_shape=[jax.ShapeDtypeStruct((N_TOK, PROJ_W), BF16), wide, wide, wide, slab, slab],
        compiler_params=_cparams("arbitrary"),
        name="input_projection",
    )(x, mods, mods, g, w, cos_t, sin_t, qg, kg)


def _pair_norm(x, gain):
    lane = lax.broadcasted_iota(jnp.int32, x.shape, 1)
    x2 = x * x
    lo = jnp.sum(jnp.where(lane < HEAD_DIM, x2, 0.0), axis=-1, keepdims=True)
    hi = jnp.sum(jnp.where(lane >= HEAD_DIM, x2, 0.0), axis=-1, keepdims=True)
    inv = jnp.where(lane < HEAD_DIM, lax.rsqrt(lo / HEAD_DIM + EPS), lax.rsqrt(hi / HEAD_DIM + EPS))
    return x * inv * gain


def _pair_rope(x, cos, sin):
    lane = lax.broadcasted_iota(jnp.int32, x.shape, 1)
    swapped = jnp.where((lane & 1) == 0, pltpu.roll(x, LANE - 1, 1), pltpu.roll(x, 1, 1))
    return x * cos + swapped * sin


def _lane_halves(x):
    lane = lax.broadcasted_iota(jnp.int32, x.shape, 1)
    xr = pltpu.roll(x, HEAD_DIM, 1)
    lo, hi = lane < HEAD_DIM, lane >= HEAD_DIM
    return (jnp.where(lo, x, 0.0), jnp.where(hi, xr, 0.0), jnp.where(lo, xr, 0.0), jnp.where(hi, x, 0.0))


def _attention_operands(q, kv, cos, sin, qg, kg, qo_ref, ko_ref, vo_ref, kn_ref, v_ref):
    for p in range(H_ATT // 2):
        qn = _pair_norm(q[:, p * LANE:(p + 1) * LANE], qg)
        qo_ref[:, p * LANE:(p + 1) * LANE] = (_pair_rope(qn, cos, sin) * (HEAD_DIM ** -0.5)).astype(BF16)
    k = _pair_norm(kv[:, 0:LANE], kg)
    v = kv[:, LANE:2 * LANE]
    kn_ref[...] = k
    v_ref[...] = v
    for j, (kx, vx) in enumerate(zip(_lane_halves(_pair_rope(k, cos, sin)), _lane_halves(v))):
        ko_ref[:, j * LANE:(j + 1) * LANE] = kx.astype(BF16)
        vo_ref[:, j * LANE:(j + 1) * LANE] = vx.astype(BF16)


def _rope_tables(tile):
    pos = jnp.arange(DEC_SEQ)
    row = (pos // GRID_W).astype(F32)
    col = (pos % GRID_W).astype(F32)
    n_freq = HEAD_DIM // 4
    inv_freq = ROPE_BASE ** (-jnp.arange(n_freq, dtype=F32) / n_freq)
    ang = jnp.concatenate([row[:, None] * inv_freq, col[:, None] * inv_freq], axis=-1)
    cos = jnp.repeat(jnp.cos(ang), 2, axis=-1)
    sin = jnp.stack([-jnp.sin(ang), jnp.sin(ang)], axis=-1).reshape(DEC_SEQ, HEAD_DIM)
    cos = jnp.concatenate([jnp.ones((tile, HEAD_DIM), F32), cos], axis=0)
    sin = jnp.concatenate([jnp.zeros((tile, HEAD_DIM), F32), sin], axis=0)
    return jnp.tile(cos, (1, 2)), jnp.tile(sin, (1, 2))


def _attend_kv_head(q2, kx, vx, sink_a, sink_b, mask, kvh):
    nq = q2.shape[0] // 2
    row = lax.broadcasted_iota(jnp.int32, (2 * nq, 1), 0)
    out = None
    for half in range(2):
        j = 2 * kvh + half
        s = lax.dot_general(q2, kx[:, j * LANE:(j + 1) * LANE], (((1,), (1,)), ((), ())), preferred_element_type=F32)
        if mask is not None:
            s = jnp.where(mask, s, NEG)
        sink = jnp.where(row < nq, sink_a[half], sink_b[half])
        m = jnp.maximum(jnp.max(s, axis=-1, keepdims=True), sink)
        e = jnp.exp(s - m)
        denom = jnp.sum(e, axis=-1, keepdims=True) + jnp.exp(sink - m)
        o = jnp.dot(e.astype(BF16), vx[:, j * LANE:(j + 1) * LANE], preferred_element_type=F32) / denom
        out = o if out is None else out + o
    return out


def _ctx_attn_kernel(sink_ref, q_ref, k_ref, v_ref, o_ref):
    kx, vx = k_ref[...], v_ref[...]
    for kvh in range(KV_HEADS):
        p0, p1 = 2 * kvh, 2 * kvh + 1
        q2 = jnp.concatenate([q_ref[:, p0 * LANE:(p0 + 1) * LANE], q_ref[:, p1 * LANE:(p1 + 1) * LANE]], axis=0)
        sa = (sink_ref[2 * p0], sink_ref[2 * p0 + 1])
        sb = (sink_ref[2 * p1], sink_ref[2 * p1 + 1])
        o = _attend_kv_head(q2, kx, vx, sa, sb, None, kvh)
        o_ref[:, p0 * LANE:(p0 + 1) * LANE] = o[:SEQ]
        o_ref[:, p1 * LANE:(p1 + 1) * LANE] = o[SEQ:]


def _context_attention(sink, qs, kx, vx):
    blk = lambda: pl.BlockSpec((SEQ, 4 * LANE), lambda b: (b, 0))
    return pl.pallas_call(
        _ctx_attn_kernel,
        grid=(BATCH,),
        in_specs=[pl.BlockSpec(memory_space=pltpu.SMEM), blk(), blk(), blk()],
        out_specs=blk(),
        out_shape=jax.ShapeDtypeStruct((P_TOK, 4 * LANE), F32),
        compiler_params=_cparams("arbitrary"),
        name="context_attention",
    )(sink, qs, kx, vx)


LAT_BLOCK = 128


def _lat_attn_kernel(sink_ref, q_ref, kp_ref, kc_ref, kn_ref, vp_ref, vc_ref, vn_ref, ck_ref, cv_ref, o_ref):
    n = pl.program_id(1)
    nb = pl.num_programs(1)
    ck = jnp.concatenate(_lane_halves(ck_ref[...]), axis=1).astype(BF16)
    cv = jnp.concatenate(_lane_halves(cv_ref[...]), axis=1).astype(BF16)
    kx = jnp.concatenate([kp_ref[...], kc_ref[...], kn_ref[...], ck], axis=0)
    vx = jnp.concatenate([vp_ref[...], vc_ref[...], vn_ref[...], cv], axis=0)
    n_keys = 3 * LAT_BLOCK + PAST_LEN
    qi = lax.broadcasted_iota(jnp.int32, (2 * LAT_BLOCK, n_keys), 0) & (LAT_BLOCK - 1)
    kj = lax.broadcasted_iota(jnp.int32, (2 * LAT_BLOCK, n_keys), 1)
    far = 2 * LAT_BLOCK
    off_prev = jnp.where(n > 0, 0, far)
    off_next = jnp.where(n < nb - 1, 0, far)
    t_prev = jnp.where(kj < LAT_BLOCK, kj - qi - off_prev, 0)
    t_next = jnp.where(kj >= 2 * LAT_BLOCK, jnp.where(kj < 3 * LAT_BLOCK, qi - off_next - (kj - 2 * LAT_BLOCK), 0), 0)
    mask = jnp.minimum(t_prev, t_next) >= 0
    for kvh in range(KV_HEADS):
        p0, p1 = 2 * kvh, 2 * kvh + 1
        q2 = jnp.concatenate([q_ref[:, p0 * LANE:(p0 + 1) * LANE], q_ref[:, p1 * LANE:(p1 + 1) * LANE]], axis=0)
        sa = (sink_ref[2 * p0], sink_ref[2 * p0 + 1])
        sb = (sink_ref[2 * p1], sink_ref[2 * p1 + 1])
        o = _attend_kv_head(q2, kx, vx, sa, sb, mask, kvh)
        o_ref[:, p0 * LANE:(p0 + 1) * LANE] = o[:LAT_BLOCK]
        o_ref[:, p1 * LANE:(p1 + 1) * LANE] = o[LAT_BLOCK:]


def _latent_attention(sink, qs, kx, vx, cache_k, cache_v, layer):
    nb = DEC_SEQ // LAT_BLOCK
    base = P_TOK // LAT_BLOCK
    cur = lambda b, n: (base + b * nb + n, 0)
    prev = lambda b, n: (base + b * nb + jnp.maximum(n - 1, 0), 0)
    nxt = lambda b, n: (base + b * nb + jnp.minimum(n + 1, nb - 1), 0)
    blk = lambda f: pl.BlockSpec((LAT_BLOCK, 4 * LANE), f)
    ctx = pl.BlockSpec((None, None, PAST_LEN, LANE), lambda b, n: (b, layer, 0, 0))
    return pl.pallas_call(
        _lat_attn_kernel,
        grid=(DEC_BATCH, nb),
        in_specs=[pl.BlockSpec(memory_space=pltpu.SMEM), blk(cur),
                  blk(prev), blk(cur), blk(nxt), blk(prev), blk(cur), blk(nxt), ctx, ctx],
        out_specs=pl.BlockSpec((LAT_BLOCK, 4 * LANE), lambda b, n: (b * nb + n, 0)),
        out_shape=jax.ShapeDtypeStruct((S_TOK, 4 * LANE), F32),
        compiler_params=_cparams("arbitrary", "arbitrary"),
        name="latent_attention",
    )(sink, qs, kx, kx, kx, vx, vx, vx, cache_k, cache_v)


DN_W = DN_HEADS * DN_DK
N_GB = 2 * 2 * DN_HEADS
DN_ROWS = DN_HEADS * DN_CHUNK
DN_TILE = SEQ
CH_PER_TILE = DN_TILE // DN_CHUNK
SEQ_GROUP = DEC_BATCH
P_GC = (BATCH // SEQ_GROUP) * (SEQ // DN_CHUNK)
N_GC = P_GC + DEC_SEQ // DN_CHUNK


def _dot_hi(a, b):
    return jnp.dot(a, b, precision=lax.Precision.HIGHEST, preferred_element_type=F32)


def _dot_nt(a, b):
    return lax.dot_general(a.astype(BF16), b.astype(BF16), (((1,), (1,)), ((), ())), preferred_element_type=F32)


def _dot_tn(a, b):
    return lax.dot_general(a.astype(BF16), b.astype(BF16), (((0,), (0,)), ((), ())), preferred_element_type=F32)


def _dot_bf(a, b):
    return jnp.dot(a.astype(BF16), b.astype(BF16), preferred_element_type=F32)


def _stack_heads(x):
    return jnp.concatenate([x[:, h * DN_DK:(h + 1) * DN_DK] for h in range(DN_HEADS)], axis=0)


def _stack_cols(m, c0):
    return jnp.concatenate([m[:, c0 + h:c0 + h + 1] for h in range(DN_HEADS)], axis=0)


def _head(x, h):
    return x[h * DN_CHUNK:(h + 1) * DN_CHUNK]


def _per_head(f):
    return jnp.concatenate([f(h) for h in range(DN_HEADS)], axis=0)


def _tri_inverses(lms, xor):
    eye = (xor == 0).astype(F32)
    ts = [eye - jnp.where(xor < 2, lm, 0.0) for lm in lms]
    lmbs = [lm.astype(BF16) for lm in lms]
    k = 1
    while (2 << k) <= DN_CHUNK:
        level = (xor >> k) == 1
        ms = [jnp.where(level, lmb, jnp.zeros_like(lmb)) for lmb in lmbs]
        tbs = [t.astype(BF16) for t in ts]
        mts = [_per_head(lambda h: _dot_bf(_head(m, h), _head(tb, h))).astype(BF16) for m, tb in zip(ms, tbs)]
        ts = [t - _per_head(lambda h: _dot_bf(_head(tb, h), _head(mt, h))) for t, tb, mt in zip(ts, tbs, mts)]
        k += 1
    return ts


def _dn_tile_slot(i):
    p_tiles = P_TOK // DN_TILE
    tiles_per_seq = DEC_SEQ // DN_TILE
    j = i - p_tiles
    blk = jnp.where(i < p_tiles, i // SEQ_GROUP, P_GC // CH_PER_TILE + j % tiles_per_seq)
    q = jnp.where(i < p_tiles, i % SEQ_GROUP, j // tiles_per_seq)
    return blk, q


def _dn_local_kernel(x_ref, xp_ref, xn_ref, ab_ref, w_ref, alog_ref, dtb_ref,
                     u_ref, w_out_ref, qd_ref, kd_ref, qk_ref, egl_ref):
    s = _silu(_dwconv3(x_ref[...].astype(F32), xp_ref[...].astype(F32)[HALO - 1:HALO, :],
                       xn_ref[...].astype(F32)[0:1, :], w_ref[...], pl.program_id(0)))
    ab = ab_ref[...].astype(F32)
    lane = lax.broadcasted_iota(jnp.int32, ab.shape, 1)
    z = ab + dtb_ref[...]
    softplus = jnp.maximum(z, 0.0) + jnp.log1p(jnp.exp(-jnp.abs(z)))
    gb = jnp.where(lane < N_GB // 2, -jnp.exp(alog_ref[...]) * softplus, jax.nn.sigmoid(ab))

    c, r = DN_CHUNK, DN_ROWS
    ri = lax.broadcasted_iota(jnp.int32, (r, c), 0) & (c - 1)
    ci = lax.broadcasted_iota(jnp.int32, (r, c), 1)
    xor = ri ^ ci
    masks = ((ri >= ci, ri > ci), (ri <= ci, ri < ci))
    tri = (lax.broadcasted_iota(jnp.int32, (c, c), 0) >= lax.broadcasted_iota(jnp.int32, (c, c), 1)).astype(F32)

    problems, lms = [], []
    for ch in range(CH_PER_TILE):
        sc = s[ch * c:(ch + 1) * c]
        qst = _stack_heads(sc[:, 0:DN_W])
        kst = _stack_heads(sc[:, DN_W:2 * DN_W])
        vst = _stack_heads(sc[:, 2 * DN_W:3 * DN_W])
        qst = qst * lax.rsqrt(jnp.sum(qst * qst, axis=-1, keepdims=True) + EPS) * (DN_DK ** -0.5)
        kst = kst * lax.rsqrt(jnp.sum(kst * kst, axis=-1, keepdims=True) + EPS)
        gbc = gb[ch * c:(ch + 1) * c]
        prefix = _dot_hi(tri, gbc)
        suffix = prefix[c - 1:c] - prefix + gbc
        qb, kb = qst.astype(BF16), kst.astype(BF16)
        gram = _per_head(lambda h: _dot_nt(_head(kb, h), _head(kb, h)))
        qk0 = _per_head(lambda h: _dot_nt(_head(qb, h), _head(kb, h)))
        for d, (gc, last) in enumerate(((prefix, c - 1), (suffix, 0))):
            incl, strict = masks[d]
            c0 = d * DN_HEADS
            gcol = _stack_cols(gc, c0)
            beta = _stack_cols(gbc, N_GB // 2 + c0)
            gct = gc.T
            grow = _per_head(lambda h: jnp.broadcast_to(gct[c0 + h:c0 + h + 1, :], (c, c)))
            decay = jnp.exp(jnp.where(incl, gcol - grow, NEG))
            lms.append(jnp.where(strict, gram * beta * decay, 0.0))
            eg = jnp.exp(gcol)
            glast = _per_head(lambda h: jnp.broadcast_to(gc[last:last + 1, c0 + h:c0 + h + 1], (c, 1)))
            qd_ref[ch, d] = (qst * eg).astype(BF16)
            kd_ref[ch, d] = (kst * jnp.exp(glast - gcol)).astype(BF16)
            qk_ref[ch, d] = jnp.where(incl, qk0 * decay, 0.0).astype(BF16)
            egl_ref[ch, d] = jnp.concatenate([jnp.broadcast_to(jnp.exp(gc[last:last + 1, c0 + h:c0 + h + 1]), (1, DN_DV))
                                              for h in range(DN_HEADS)], axis=1)
            problems.append((ch, d, jnp.concatenate([vst * beta, kst * (beta * eg)], axis=1).astype(BF16)))

    for (ch, d, rhs), t in zip(problems, _tri_inverses(lms, xor)):
        tb = t.astype(BF16)
        x = _per_head(lambda h: jnp.dot(_head(tb, h), _head(rhs, h), preferred_element_type=F32))
        u_ref[ch, d] = x[:, 0:DN_DV]
        w_out_ref[ch, d] = x[:, DN_DV:2 * DN_DV].astype(BF16)


def _dn_local(proj, conv_w, alog, dtb):
    wq = 3 * DN_W
    prev, nxt = _halo_specs(DN_TILE, wq, COL_DQKV // wq)

    def slot_spec(*tail):
        return pl.BlockSpec((CH_PER_TILE, None, 2) + tail,
                            lambda i: _dn_tile_slot(i) + (0,) * (1 + len(tail)))

    def slot_shape(dtype, *tail):
        return jax.ShapeDtypeStruct((N_GC, SEQ_GROUP, 2) + tail, dtype)

    return pl.pallas_call(
        _dn_local_kernel,
        grid=(N_TOK // DN_TILE,),
        in_specs=[pl.BlockSpec((DN_TILE, wq), lambda i: (i, COL_DQKV // wq)), prev, nxt,
                  pl.BlockSpec((DN_TILE, LANE), lambda i: (i, COL_AB // LANE)),
                  pl.BlockSpec((3, wq), lambda i: (0, 0)),
                  pl.BlockSpec((1, LANE), lambda i: (0, 0)),
                  pl.BlockSpec((1, LANE), lambda i: (0, 0))],
        out_specs=[slot_spec(DN_ROWS, DN_DV), slot_spec(DN_ROWS, DN_DK), slot_spec(DN_ROWS, DN_DK),
                   slot_spec(DN_ROWS, DN_DK), slot_spec(DN_ROWS, DN_CHUNK), slot_spec(1, DN_W)],
        out_shape=[slot_shape(F32, DN_ROWS, DN_DV), slot_shape(BF16, DN_ROWS, DN_DK), slot_shape(BF16, DN_ROWS, DN_DK),
                   slot_shape(BF16, DN_ROWS, DN_DK), slot_shape(BF16, DN_ROWS, DN_CHUNK), slot_shape(F32, 1, DN_W)],
        compiler_params=_cparams("arbitrary"),
        name="deltanet_local",
    )(proj, proj, proj, proj, conv_w, alog, dtb)


def _scan_step(t):
    pc, sc = SEQ // DN_CHUNK, DEC_SEQ // DN_CHUNK
    is_p = t < P_GC
    g = jnp.where(is_p, t // pc, BATCH // SEQ_GROUP)
    s = jnp.where(is_p, t % pc, t - P_GC)
    n = jnp.where(is_p, pc, sc)
    return g, s, n, t, t - s + (n - 1 - s)


def _dn_scan_kernel(uf_ref, wf_ref, qdf_ref, kdf_ref, qkf_ref, egf_ref, ub_ref, wb_ref, qdb_ref, kdb_ref, qkb_ref,
                    egb_ref, s0f_ref, s0b_ref, of_ref, ob_ref, sf_ref, sb_ref, state_ref):
    _, s, n, _, _ = _scan_step(pl.program_id(0))
    c, r = DN_CHUNK, DN_ROWS
    heads = range(DN_HEADS)

    @pl.when(s == 0)
    def _():
        for q in range(SEQ_GROUP):
            state_ref[0, q] = jnp.concatenate([s0f_ref[q, h] for h in heads], axis=1)
            state_ref[1, q] = jnp.concatenate([s0b_ref[q, h] for h in heads], axis=1)

    dirs = ((uf_ref, wf_ref, qdf_ref, kdf_ref, qkf_ref, egf_ref, of_ref),
            (ub_ref, wb_ref, qdb_ref, kdb_ref, qkb_ref, egb_ref, ob_ref))
    chains = [(d, q) + refs for d, refs in enumerate(dirs) for q in range(SEQ_GROUP)]
    states = [state_ref[d, q] for d, q, *_ in chains]
    def state_products(w, qd, st):
        sb = st.astype(BF16)
        both = [jnp.dot(jnp.concatenate([_head(w, h), _head(qd, h)], axis=0), sb[:, h * DN_DV:(h + 1) * DN_DV],
                        preferred_element_type=F32) for h in heads]
        return (jnp.concatenate([b[0:c] for b in both], axis=0), jnp.concatenate([b[c:2 * c] for b in both], axis=0))

    boths = [state_products(w_ref[q], qd_ref[q], st) for (d, q, u_ref, w_ref, qd_ref, *_), st in zip(chains, states)]
    v_news = [(u_ref[q] - ws).astype(BF16) for (d, q, u_ref, *_), (ws, _) in zip(chains, boths)]
    for (d, q, u_ref, w_ref, qd_ref, kd_ref, qk_ref, eg_ref, o_ref), st, (_, qs), v_new in zip(chains, states, boths, v_news):
        qk = qk_ref[q]
        o = qs + _per_head(lambda h: jnp.dot(_head(qk, h), _head(v_new, h), preferred_element_type=F32))
        kd = kd_ref[q]
        outer = jnp.concatenate([lax.dot_general(_head(kd, h), _head(v_new, h), (((0,), (0,)), ((), ())),
                                                 preferred_element_type=F32) for h in heads], axis=1)
        state_ref[d, q] = st * eg_ref[q] + outer
        o_ref[q] = jnp.concatenate([o[h * c:(h + 1) * c] for h in heads], axis=1)

    @pl.when(s == n - 1)
    def _():
        for q in range(SEQ_GROUP):
            for h in heads:
                sf_ref[q, h] = state_ref[0, q][:, h * DN_DV:(h + 1) * DN_DV]
                sb_ref[q, h] = state_ref[1, q][:, h * DN_DV:(h + 1) * DN_DV]


def _dn_scan(u, w, qd, kd, qk, egl, s0f, s0b):
    fwd = lambda t: _scan_step(t)[3]
    bwd = lambda t: _scan_step(t)[4]
    grp = lambda t: _scan_step(t)[0]

    def specs(slot, d):
        sp = lambda *tail: pl.BlockSpec((None, SEQ_GROUP, None) + tail, lambda t: (slot(t), 0, d) + (0,) * len(tail))
        return [sp(DN_ROWS, DN_DV), sp(DN_ROWS, DN_DK), sp(DN_ROWS, DN_DK), sp(DN_ROWS, DN_DK), sp(DN_ROWS, DN_CHUNK),
                sp(1, DN_W)]

    st = lambda: pl.BlockSpec((SEQ_GROUP, DN_HEADS, DN_DK, DN_DV), lambda t: (grp(t), 0, 0, 0))
    o_spec = lambda slot: pl.BlockSpec((None, SEQ_GROUP, DN_CHUNK, DN_W), lambda t: (slot(t), 0, 0, 0))
    o_shape = jax.ShapeDtypeStruct((N_GC, SEQ_GROUP, DN_CHUNK, DN_W), F32)
    st_shape = jax.ShapeDtypeStruct((N_SEQ, DN_HEADS, DN_DK, DN_DV), F32)
    args = (u, w, qd, kd, qk, egl)
    return pl.pallas_call(
        _dn_scan_kernel,
        grid=(N_GC,),
        in_specs=specs(fwd, 0) + specs(bwd, 1) + [st(), st()],
        out_specs=[o_spec(fwd), o_spec(bwd), st(), st()],
        out_shape=[o_shape, o_shape, st_shape, st_shape],
        scratch_shapes=[pltpu.VMEM((2, SEQ_GROUP, DN_DK, DN_W), F32)],
        compiler_params=_cparams("arbitrary"),
        name="deltanet_scan",
    )(*args, *args, s0f, s0b)


def _merge_kernel(a_ref, ap_ref, an_ref, yb_ref, of_ref, ob_ref, z_ref, ga_ref, gb_ref, gc_ref, x_ref, g1_ref,
                  cw_ref, ng_ref, wa_ref, wb_ref, wc_ref, wo_ref, o_ref):
    tm = x_ref.shape[0]
    w = A_WIDTH
    a = a_ref[...].astype(F32)
    ap = ap_ref[...].astype(F32)[HALO - 1:HALO, :]
    an = an_ref[...].astype(F32)[0:1, :]
    u = a[:, w:2 * w] * a[:, 2 * w:3 * w]
    up = ap[:, w:2 * w] * ap[:, 2 * w:3 * w]
    un = an[:, w:2 * w] * an[:, 2 * w:3 * w]
    y_a = a[:, 0:w] * _dwconv3(u, up, un, cw_ref[...], pl.program_id(0))
    merged = _sigmoid(ga_ref[...].astype(F32)) * _dot_bf(y_a, wa_ref[...])
    merged += _sigmoid(gb_ref[...].astype(F32)) * _dot_bf(yb_ref[...], wb_ref[...])
    o_sum = (of_ref[...] + ob_ref[...]).reshape(tm, DN_W)
    parts = []
    for h in range(DN_HEADS):
        sl = slice(h * DN_DV, (h + 1) * DN_DV)
        o = o_sum[:, sl]
        o = o * lax.rsqrt(jnp.mean(o * o, axis=-1, keepdims=True) + EPS) * ng_ref[...]
        parts.append(o * _silu(z_ref[:, sl].astype(F32)))
    y_c = jnp.concatenate(parts, axis=1)
    merged += _sigmoid(gc_ref[...].astype(F32)) * _dot_bf(y_c, wc_ref[...])
    o_ref[...] = x_ref[...] + g1_ref[...] * _dot_bf(merged, wo_ref[...])


def _merge(proj, y_b, o_f, o_b, x, mods, layer, conv_a_w, dn_norm_g, wa, wb, wc, wo):
    tm = DN_TILE
    wa3 = 3 * A_WIDTH
    prev, nxt = _halo_specs(tm, wa3, COL_A // wa3)
    tok = lambda w, cb=0: pl.BlockSpec((tm, w), lambda i: (i, cb))
    full = lambda a: pl.BlockSpec(a.shape, lambda i: (0,) * a.ndim)
    scan_out = lambda: pl.BlockSpec((CH_PER_TILE, None, DN_CHUNK, DN_W), lambda i: _dn_tile_slot(i) + (0, 0))
    gate0 = COL_GATES // D_MODEL
    return pl.pallas_call(
        _merge_kernel,
        grid=(N_TOK // tm,),
        in_specs=[tok(wa3, COL_A // wa3), prev, nxt, tok(A_WIDTH), scan_out(), scan_out(), tok(DN_W, COL_DZ // DN_W),
                  tok(D_MODEL, gate0), tok(D_MODEL, gate0 + 1), tok(D_MODEL, gate0 + 2), tok(D_MODEL),
                  _mod_spec(layer, 2, tm), full(conv_a_w), full(dn_norm_g), full(wa), full(wb), full(wc), full(wo)],
        out_specs=tok(D_MODEL),
        out_shape=jax.ShapeDtypeStruct((N_TOK, D_MODEL), F32),
        compiler_params=_cparams("arbitrary"),
        name="mixer_merge",
    )(proj, proj, proj, y_b, o_f, o_b, proj, proj, proj, proj, x, mods, conv_a_w, dn_norm_g, wa, wb, wc, wo)


FF_CHUNK = 256


def _swiglu_chunked(h, wg_ref, wu_ref, wd_ref):
    width = wg_ref.shape[1]
    bounds = list(range(0, width, FF_CHUNK)) + [width]
    if bounds[-1] - bounds[-2] < FF_CHUNK and len(bounds) > 2:
        del bounds[-2]
    chunks = list(zip(bounds[:-1], bounds[1:]))
    gate_up = lambda lo, hi: (jnp.dot(h, wg_ref[:, lo:hi], preferred_element_type=F32),
                              jnp.dot(h, wu_ref[:, lo:hi], preferred_element_type=F32))
    out = None
    nxt = gate_up(*chunks[0])
    for c, (lo, hi) in enumerate(chunks):
        g, u = nxt
        if c + 1 < len(chunks):
            nxt = gate_up(*chunks[c + 1])
        y = jnp.dot((_silu(g) * u).astype(BF16), wd_ref[lo:hi, :], preferred_element_type=F32)
        out = y if out is None else out + y
    return out


def _ffn_kernel(x_ref, sh_ref, sc_ref, g2_ref, g_ref, wg_ref, wu_ref, wd_ref, o_ref, h_ref, acc_ref):
    f = pl.program_id(1)

    @pl.when(f == 0)
    def _():
        h_ref[...] = _modulated_norm(x_ref[...], g_ref[...], sh_ref[...], sc_ref[...]).astype(BF16)
        acc_ref[...] = jnp.zeros_like(acc_ref)

    acc_ref[...] += _swiglu_chunked(h_ref[...], wg_ref, wu_ref, wd_ref)

    @pl.when(f == pl.num_programs(1) - 1)
    def _():
        o_ref[...] = x_ref[...] + g2_ref[...] * acc_ref[...]


def _dense_ffn(x, mods, layer, g, wg, wu, wd):
    tm, tf = 1024, 1408
    return pl.pallas_call(
        _ffn_kernel,
        grid=(N_TOK // tm, FF_DENSE // tf),
        in_specs=[pl.BlockSpec((tm, D_MODEL), lambda i, f: (i, 0)),
                  _mod_spec(layer, 3, tm), _mod_spec(layer, 4, tm), _mod_spec(layer, 5, tm),
                  pl.BlockSpec((1, D_MODEL), lambda i, f: (0, 0)),
                  pl.BlockSpec((D_MODEL, tf), lambda i, f: (0, f)),
                  pl.BlockSpec((D_MODEL, tf), lambda i, f: (0, f)),
                  pl.BlockSpec((tf, D_MODEL), lambda i, f: (f, 0))],
        out_specs=pl.BlockSpec((tm, D_MODEL), lambda i, f: (i, 0)),
        out_shape=jax.ShapeDtypeStruct((N_TOK, D_MODEL), F32),
        scratch_shapes=[pltpu.VMEM((tm, D_MODEL), BF16), pltpu.VMEM((tm, D_MODEL), F32)],
        compiler_params=_cparams("arbitrary", "arbitrary"),
        name="dense_ffn",
    )(x, mods, mods, mods, g, wg, wu, wd)


MOE_TM = 1024
MOE_BLOCKS = ((0, 256), (256, 128), (384, 128), (512, 256), (768, 256))
ROUTE_ROWS = 16


def _moe_kernel(x_ref, sh_ref, sc_ref, g2_ref, g_ref, wr_ref, br_ref, before_ref, wg_ref, wu_ref, wd_ref, o_ref,
                h_ref, route_ref, acc_ref):
    e = pl.program_id(1)
    sub = lax.broadcasted_iota(jnp.int32, (ROUTE_ROWS, MOE_TM), 0)

    @pl.when(e == 0)
    def _():
        hb = _modulated_norm(x_ref[...], g_ref[...], sh_ref[...], sc_ref[...]).astype(BF16)
        h_ref[...] = hb
        logits = lax.dot_general(wr_ref[...].astype(BF16), hb, (((1,), (1,)), ((), ())),
                                 preferred_element_type=F32) + br_ref[:, 0:1]
        m1 = jnp.max(logits, axis=0, keepdims=True)
        i1 = jnp.min(jnp.where(logits == m1, sub, ROUTE_ROWS), axis=0, keepdims=True)
        rest = jnp.where(sub == i1, -jnp.inf, logits)
        m2 = jnp.max(rest, axis=0, keepdims=True)
        i2 = jnp.min(jnp.where(rest == m2, sub, ROUTE_ROWS), axis=0, keepdims=True)
        e2 = jnp.exp(m2 - m1)
        route_ref[0] = jnp.where(sub == i1, 1.0 / (1.0 + e2), 0.0) + jnp.where(sub == i2, e2 / (1.0 + e2), 0.0)
        sel = jnp.where(sub == i1, 1.0, jnp.where(sub == i2, 1.0, 0.0))
        rank = jnp.dot(sel.astype(BF16), before_ref[...], preferred_element_type=F32)
        route_ref[1] = jnp.where(sel > 0.5, rank, -1.0)
        acc_ref[...] = jnp.zeros_like(acc_ref)

    pick = sub == e
    comb_e = jnp.sum(jnp.where(pick, route_ref[0], 0.0), axis=0, keepdims=True)
    key_e = jnp.sum(jnp.where(pick, route_ref[1], 0.0), axis=0, keepdims=True)
    count = jnp.sum(jnp.where(key_e >= 0.0, 1.0, 0.0))

    for start, rows in MOE_BLOCKS:
        @pl.when(count > start)
        def _():
            r = (lax.broadcasted_iota(jnp.int32, (rows, MOE_TM), 0) + start).astype(F32)
            hit = key_e == r
            onehot = jnp.where(hit, 1.0, 0.0).astype(BF16)
            hs = jnp.dot(onehot, h_ref[...], preferred_element_type=F32).astype(BF16)
            y = _swiglu_chunked(hs, wg_ref, wu_ref, wd_ref)
            w_row = jnp.sum(jnp.where(hit, comb_e, 0.0), axis=1, keepdims=True)
            acc_ref[...] += lax.dot_general(onehot, (y * w_row).astype(BF16), (((0,), (0,)), ((), ())),
                                            preferred_element_type=F32)

    @pl.when(e == pl.num_programs(1) - 1)
    def _():
        o_ref[...] = x_ref[...] + g2_ref[...] * acc_ref[...]


def _moe_ffn(x, mods, layer, g, wr, br, wg, wu, wd):
    tm = MOE_TM
    ew = lambda: pl.BlockSpec((None, D_MODEL, FF_EXPERT), lambda i, e: (e, 0, 0))
    before = jnp.triu(jnp.ones((tm, tm), BF16), k=1)
    return pl.pallas_call(
        _moe_kernel,
        grid=(N_TOK // tm, N_EXPERTS),
        in_specs=[pl.BlockSpec((tm, D_MODEL), lambda i, e: (i, 0)),
                  _mod_spec(layer, 3, tm), _mod_spec(layer, 4, tm), _mod_spec(layer, 5, tm),
                  pl.BlockSpec((1, D_MODEL), lambda i, e: (0, 0)),
                  pl.BlockSpec((ROUTE_ROWS, D_MODEL), lambda i, e: (0, 0)),
                  pl.BlockSpec((ROUTE_ROWS, LANE), lambda i, e: (0, 0)),
                  pl.BlockSpec((tm, tm), lambda i, e: (0, 0)),
                  ew(), ew(), ew()],
        out_specs=pl.BlockSpec((tm, D_MODEL), lambda i, e: (i, 0)),
        out_shape=jax.ShapeDtypeStruct((N_TOK, D_MODEL), F32),
        scratch_shapes=[pltpu.VMEM((tm, D_MODEL), BF16), pltpu.VMEM((2, ROUTE_ROWS, tm), F32),
                        pltpu.VMEM((tm, D_MODEL), F32)],
        compiler_params=_cparams("arbitrary", "arbitrary"),
        name="moe_ffn",
    )(x, mods, mods, mods, g, wr, br, before, wg, wu, wd)


def _reorder_w_in(w):
    pad = jnp.zeros((w.shape[0], PROJ_W - (COL_AB + N_GB)), w.dtype)
    return jnp.concatenate([w[:, 0:1536], w[:, 2304:3840], w[:, 1536:2048], w[:, 3840:4352], w[:, 4368:7440],
                            w[:, 2048:2304], w[:, 4352:4368], pad], axis=1).astype(BF16)


def _pad_lanes(v, fill=0.0):
    v = v.reshape(1, -1)
    return jnp.concatenate([v, jnp.full((1, LANE - v.shape[1]), fill, v.dtype)], axis=1)


def kernel(x_prompt, x_sample, cache_k, cache_v, state_fwd, state_bwd, c, c_ctx, norm1_g, norm2_g, w_ada, b_ada, w_in, conv_a_w, q_norm_g, k_norm_g, attn_sink, dn_conv_w, dn_a_log, dn_dt_bias, dn_norm_g, w_out_a, w_out_b, w_out_c, w_o, w_ff_gate, w_ff_up, w_ff_down, w_router, b_router, w_moe_gate, w_moe_up, w_moe_down):
    x = jnp.concatenate([x_prompt.reshape(P_TOK, D_MODEL), x_sample.reshape(S_TOK, D_MODEL)], axis=0)
    cmat = jnp.concatenate([c_ctx[None, :], c, jnp.zeros((SUBLANE - 1 - DEC_BATCH, D_MODEL), F32)], axis=0)
    mods = _modulations(cmat, w_ada, b_ada)
    cos_t, sin_t = _rope_tables(PROJ_TM)
    ck = cache_k.reshape(DEC_BATCH, DEPTH, PAST_LEN, LANE)
    cv = cache_v.reshape(DEC_BATCH, DEPTH, PAST_LEN, LANE)
    zero_state = jnp.zeros((BATCH, DN_HEADS, DN_DK, DN_DV), F32)

    ks_out, vs_out, sf_out, sb_out = [], [], [], []
    for l in range(DEPTH):
        proj, qs, kx, vx, k_norm, v_raw = _input_projection(
            x, mods, l, norm1_g[l][None, :], _reorder_w_in(w_in[l]), cos_t, sin_t,
            jnp.tile(q_norm_g[l], 2)[None, :], jnp.tile(k_norm_g[l], 2)[None, :])
        y_b = jnp.concatenate([_context_attention(attn_sink[l], qs, kx, vx),
                               _latent_attention(attn_sink[l], qs, kx, vx, ck, cv, l)], axis=0)
        ks_out.append(k_norm[:P_TOK].reshape(BATCH, SEQ, KV_HEADS, HEAD_DIM))
        vs_out.append(v_raw[:P_TOK].reshape(BATCH, SEQ, KV_HEADS, HEAD_DIM))

        local = _dn_local(proj, dn_conv_w[l], _pad_lanes(dn_a_log[l]), _pad_lanes(dn_dt_bias[l]))
        s0f = jnp.concatenate([zero_state, state_fwd[:, l]], axis=0)
        s0b = jnp.concatenate([zero_state, state_bwd[:, l]], axis=0)
        o_f, o_b, s_f, s_b = _dn_scan(*local, s0f, s0b)
        sf_out.append(s_f[:BATCH])
        sb_out.append(s_b[:BATCH])

        x = _merge(proj, y_b, o_f, o_b, x, mods, l, conv_a_w[l], dn_norm_g[l][None, :],
                   w_out_a[l].astype(BF16), w_out_b[l].astype(BF16), w_out_c[l].astype(BF16), w_o[l].astype(BF16))

        i = l // 2
        if l % 2 == 0:
            x = _dense_ffn(x, mods, l, norm2_g[l][None, :], w_ff_gate[i].astype(BF16), w_ff_up[i].astype(BF16),
                           w_ff_down[i].astype(BF16))
        else:
            wr = jnp.concatenate([w_router[i].T, jnp.zeros((ROUTE_ROWS - N_EXPERTS, D_MODEL), F32)], axis=0)
            br = jnp.concatenate([b_router[i], jnp.full((ROUTE_ROWS - N_EXPERTS,), -jnp.inf, F32)])
            x = _moe_ffn(x, mods, l, norm2_g[l][None, :], wr, jnp.broadcast_to(br[:, None], (ROUTE_ROWS, LANE)),
                         w_moe_gate[i].astype(BF16), w_moe_up[i].astype(BF16), w_moe_down[i].astype(BF16))

    return (x[:P_TOK].reshape(BATCH, SEQ, D_MODEL), x[P_TOK:].reshape(DEC_BATCH, DEC_SEQ, D_MODEL),
            jnp.stack(ks_out, axis=1), jnp.stack(vs_out, axis=1), jnp.stack(sf_out, axis=1), jnp.stack(sb_out, axis=1))
```
